```python
import math
import jax, jax.numpy as jnp
from jax import lax
import numpy as np

D_MODEL = 1024
BATCH = 8
SEQ = 8192
DEPTH = 1

N_META = 16
N_ATTN_HEADS = 16
QK_NOPE_DIM = 64
QK_ROPE_DIM = 32
QK_HEAD_DIM = QK_NOPE_DIM + QK_ROPE_DIM
V_HEAD_DIM = 64
Q_LORA_RANK = 384
KV_LORA_RANK = 256
ROPE_THETA = 10000.0
ATTN_WIDTH = N_ATTN_HEADS * V_HEAD_DIM
Q_BLOCK = 128
SSD_EXPAND = 2
SSD_WIDTH = SSD_EXPAND * D_MODEL
SSD_HEAD_DIM = 64
N_SSD_HEADS = SSD_WIDTH // SSD_HEAD_DIM
N_SSD_GROUPS = 4
D_STATE = 128
CONV_WIDTH = 5
CONV_DIM = SSD_WIDTH + 2 * N_SSD_GROUPS * D_STATE
CHUNK = 128
N_BRANCHES = 2
NORM_EPS = 1e-6
IN_SPLITS = (Q_LORA_RANK, KV_LORA_RANK, QK_ROPE_DIM, ATTN_WIDTH, SSD_WIDTH, CONV_DIM, 2 * N_SSD_HEADS, N_BRANCHES * D_MODEL)
IN_DIM = Q_LORA_RANK + KV_LORA_RANK + QK_ROPE_DIM + ATTN_WIDTH + SSD_WIDTH + CONV_DIM + 2 * N_SSD_HEADS + N_BRANCHES * D_MODEL

kernel_name = "hybrid_mla_ssd_meta_encoder"


def rmsnorm(x, w):
    xf = x.astype(jnp.float32)
    y = xf * lax.rsqrt(jnp.mean(xf * xf, axis=-1, keepdims=True) + NORM_EPS)
    return (y * w.astype(jnp.float32)).astype(x.dtype)


def rope(x, cos, sin):
    xf = x.astype(jnp.float32)
    half = xf.shape[-1] // 2
    x1, x2 = xf[..., :half], xf[..., half:]
    return jnp.concatenate([x1 * cos - x2 * sin, x2 * cos + x1 * sin], axis=-1).astype(x.dtype)


def mla_branch(q_lat, kv_lat, k_rope, cos, sin, q_norm_w, w_uq, kv_norm_w, w_ukv):
    Bz, L, _ = q_lat.shape
    q = (rmsnorm(q_lat, q_norm_w) @ w_uq).reshape(Bz, L, N_ATTN_HEADS, QK_HEAD_DIM)
    kv = (rmsnorm(kv_lat, kv_norm_w) @ w_ukv).reshape(Bz, L, N_ATTN_HEADS, QK_NOPE_DIM + V_HEAD_DIM)
    q_nope, q_pe = q[..., :QK_NOPE_DIM], q[..., QK_NOPE_DIM:]
    k_nope, v = kv[..., :QK_NOPE_DIM], kv[..., QK_NOPE_DIM:]
    q_pe = rope(q_pe, cos[:, None, :], sin[:, None, :])
    k_pe = rope(k_rope, cos, sin)
    k_pe = jnp.broadcast_to(k_pe[:, :, None, :], (Bz, L, N_ATTN_HEADS, QK_ROPE_DIM))
    q = jnp.concatenate([q_nope, q_pe], axis=-1)
    k = jnp.concatenate([k_nope, k_pe], axis=-1)
    n_blk = -(-L // Q_BLOCK)
    q_pad = n_blk * Q_BLOCK - L
    qb = jnp.pad(q, ((0, 0), (0, q_pad), (0, 0), (0, 0)))
    qb = jnp.swapaxes(qb.reshape(Bz, n_blk, Q_BLOCK, N_ATTN_HEADS, QK_HEAD_DIM), 0, 1)
    scale = QK_HEAD_DIM ** -0.5

    def attend(q_blk):
        s = jnp.einsum('bqhd,bkhd->bhqk', q_blk, k).astype(jnp.float32) * scale
        p = jax.nn.softmax(s, axis=-1)
        return jnp.einsum('bhqk,bkhd->bqhd', p.astype(v.dtype), v)

    o = lax.map(attend, qb)
    o = jnp.swapaxes(o, 0, 1).reshape(Bz, n_blk * Q_BLOCK, ATTN_WIDTH)[:, :L]
    return o


def ssd_chunked(x, dt, a, b, c):
    x = x.astype(jnp.float32); dt = dt.astype(jnp.float32)
    b = b.astype(jnp.float32); c = c.astype(jnp.float32)
    Bz, T, H, P = x.shape
    G, N = b.shape[2], b.shape[3]
    J = H // G
    nc = T // CHUNK
    dt_c = dt.reshape(Bz, nc, CHUNK, G, J)
    xdt = x.reshape(Bz, nc, CHUNK, G, J, P) * dt_c[..., None]
    bc = b.reshape(Bz, nc, CHUNK, G, N)
    cc = c.reshape(Bz, nc, CHUNK, G, N)
    dA = jnp.moveaxis(dt_c * a.reshape(G, J), 2, -1)
    acs = jnp.cumsum(dA, axis=-1)
    mask = jnp.tril(jnp.ones((CHUNK, CHUNK), dtype=bool))
    seg = acs[..., :, None] - acs[..., None, :]
    decay = jnp.exp(jnp.where(mask, seg, -jnp.inf))
    cb = jnp.einsum('bclgn,bcsgn->bcgls', cc, bc)
    m = cb[:, :, :, None] * decay
    y_diag = jnp.einsum('bcgjls,bcsgjp->bclgjp', m, xdt)
    decay_states = jnp.moveaxis(jnp.exp(acs[..., -1:] - acs), -1, 2)
    states = jnp.einsum('bclgn,bclgjp->bcgjpn', bc, xdt * decay_states[..., None])
    chunk_decay = jnp.exp(acs[..., -1])

    def step(h, inp):
        s, d = inp
        return h * d[..., None, None] + s, h

    h0 = jnp.zeros((Bz, G, J, P, N), jnp.float32)
    _, h_in = lax.scan(step, h0, (jnp.moveaxis(states, 1, 0), jnp.moveaxis(chunk_decay, 1, 0)))
    h_in = jnp.moveaxis(h_in, 0, 1)
    state_decay = jnp.moveaxis(jnp.exp(acs), -1, 2)
    y_off = jnp.einsum('bclgn,bcgjpn->bclgjp', cc, h_in) * state_decay[..., None]
    return (y_diag + y_off).reshape(Bz, T, H, P)


def ssd_branch(z, xbc, dt_raw, conv_w, conv_b, dt_bias, a_log, d_skip, ssd_norm_w):
    Bz, L, _ = xbc.shape
    xbc = lax.conv_general_dilated(xbc, conv_w[:, None, :], window_strides=(1,),
                                   padding=((CONV_WIDTH // 2, CONV_WIDTH // 2),),
                                   dimension_numbers=('NWC', 'WIO', 'NWC'),
                                   feature_group_count=CONV_DIM)
    xbc = jax.nn.silu(xbc + conv_b)
    gn = N_SSD_GROUPS * D_STATE
    xs, bs, cs = xbc[..., :SSD_WIDTH], xbc[..., SSD_WIDTH:SSD_WIDTH + gn], xbc[..., SSD_WIDTH + gn:]
    dt = jax.nn.softplus(dt_raw.astype(jnp.float32).reshape(Bz, L, 2, N_SSD_HEADS) + dt_bias.astype(jnp.float32))
    a = -jnp.exp(a_log.astype(jnp.float32))
    pad = (-N_META) % CHUNK
    fpad = lambda t: jnp.pad(t, ((0, 0), (pad, 0)) + ((0, 0),) * (t.ndim - 2))
    T = L + pad
    xs_p = fpad(xs).reshape(Bz, T, N_SSD_HEADS, SSD_HEAD_DIM)
    b_p = fpad(bs).reshape(Bz, T, N_SSD_GROUPS, D_STATE)
    c_p = fpad(cs).reshape(Bz, T, N_SSD_GROUPS, D_STATE)
    dt_p = fpad(dt)
    flip = lambda t: jnp.flip(t, axis=1)
    y_f = ssd_chunked(xs_p, dt_p[:, :, 0], a[0], b_p, c_p)
    y_b = flip(ssd_chunked(flip(xs_p), flip(dt_p[:, :, 1]), a[1], flip(b_p), flip(c_p)))
    x4 = xs.reshape(Bz, L, N_SSD_HEADS, SSD_HEAD_DIM).astype(jnp.float32)
    y = (y_f + y_b)[:, pad:] + d_skip.astype(jnp.float32)[:, None] * x4
    y = y.reshape(Bz, L, SSD_WIDTH) * jax.nn.silu(z.astype(jnp.float32))
    y = rmsnorm(y.reshape(Bz, L, N_SSD_GROUPS, SSD_WIDTH // N_SSD_GROUPS),
                ssd_norm_w.reshape(N_SSD_GROUPS, SSD_WIDTH // N_SSD_GROUPS))
    return y.reshape(Bz, L, SSD_WIDTH).astype(xs.dtype)


def setup_inputs(seed: int = 0) -> dict:
    key = jax.random.key(seed)
    ks = jax.random.split(key, 20)
    f32 = jnp.float32
    nrm = lambda k, shape, s: jax.random.normal(k, shape, f32) * s
    gain = lambda k, shape: 1.0 + 0.02 * jax.random.normal(k, shape, f32)
    u = jax.random.uniform(ks[10], (DEPTH, 2, N_SSD_HEADS), f32)
    dt0 = jnp.exp(u * (math.log(0.1) - math.log(0.001)) + math.log(0.001))
    dt_bias = dt0 + jnp.log(-jnp.expm1(-dt0))
    a_log = jnp.log(jax.random.uniform(ks[11], (DEPTH, 2, N_SSD_HEADS), f32, 1.0, 16.0))
    return {
        "x": jax.random.normal(ks[0], (BATCH, SEQ, D_MODEL), f32),
        "meta_tokens": nrm(ks[1], (N_META, D_MODEL), 1.0),
        "norm_w": gain(ks[2], (DEPTH, D_MODEL)),
        "w_in": nrm(ks[3], (DEPTH, D_MODEL, IN_DIM), D_MODEL ** -0.5),
        "q_norm_w": gain(ks[4], (DEPTH, Q_LORA_RANK)),
        "w_uq": nrm(ks[5], (DEPTH, Q_LORA_RANK, N_ATTN_HEADS * QK_HEAD_DIM), Q_LORA_RANK ** -0.5),
        "kv_norm_w": gain(ks[6], (DEPTH, KV_LORA_RANK)),
        "w_ukv": nrm(ks[7], (DEPTH, KV_LORA_RANK, N_ATTN_HEADS * (QK_NOPE_DIM + V_HEAD_DIM)), KV_LORA_RANK ** -0.5),
        "w_attn_proj": nrm(ks[8], (DEPTH, ATTN_WIDTH, D_MODEL), ATTN_WIDTH ** -0.5),
        "conv_w": nrm(ks[9], (DEPTH, CONV_WIDTH, CONV_DIM), CONV_WIDTH ** -0.5),
        "conv_b": nrm(ks[12], (DEPTH, CONV_DIM), 0.02),
        "dt_bias": dt_bias,
        "a_log": a_log,
        "d_skip": 1.0 + 0.1 * jax.random.normal(ks[13], (DEPTH, N_SSD_HEADS), f32),
        "ssd_norm_w": gain(ks[14], (DEPTH, SSD_WIDTH)),
        "w_ssd_proj": nrm(ks[15], (DEPTH, SSD_WIDTH, D_MODEL), SSD_WIDTH ** -0.5),
        "w_out": nrm(ks[16], (DEPTH, D_MODEL, D_MODEL), D_MODEL ** -0.5),
        "final_norm_w": gain(ks[17], (D_MODEL,)),
    }


def reference(x, meta_tokens, norm_w, w_in, q_norm_w, w_uq, kv_norm_w, w_ukv, w_attn_proj,
              conv_w, conv_b, dt_bias, a_log, d_skip, ssd_norm_w, w_ssd_proj, w_out, final_norm_w):
    Bz = x.shape[0]
    meta = jnp.broadcast_to(meta_tokens[None].astype(x.dtype), (Bz, N_META, D_MODEL))
    h = jnp.concatenate([meta, x], axis=1)
    L = h.shape[1]
    pos = jnp.arange(L, dtype=jnp.float32)
    inv_freq = ROPE_THETA ** (-jnp.arange(0, QK_ROPE_DIM, 2, dtype=jnp.float32) / QK_ROPE_DIM)
    ang = pos[:, None] * inv_freq[None, :]
    cos, sin = jnp.cos(ang), jnp.sin(ang)
    offsets = [int(o) for o in np.cumsum(IN_SPLITS)[:-1]]
    for l in range(DEPTH):
        u = rmsnorm(h, norm_w[l])
        proj = u @ w_in[l]
        q_lat, kv_lat, k_rope, g_attn, z, xbc, dt_raw, merge = jnp.split(proj, offsets, axis=-1)
        y_a = mla_branch(q_lat, kv_lat, k_rope, cos, sin, q_norm_w[l], w_uq[l], kv_norm_w[l], w_ukv[l])
        y_a = (y_a * jax.nn.silu(g_attn)) @ w_attn_proj[l]
        y_s = ssd_branch(z, xbc, dt_raw, conv_w[l], conv_b[l], dt_bias[l], a_log[l], d_skip[l], ssd_norm_w[l])
        y_s = y_s @ w_ssd_proj[l]
        gates = jax.nn.sigmoid(merge).reshape(Bz, L, N_BRANCHES, D_MODEL)
        mixed = gates[:, :, 0] * y_a + gates[:, :, 1] * y_s
        h = h + mixed @ w_out[l]
    h = rmsnorm(h, final_norm_w)
    return h[:, N_META:]
```

```python
import functools
import math

import jax
import jax.numpy as jnp
from jax import lax
from jax.experimental import pallas as pl
from jax.experimental.pallas import tpu as pltpu

F32 = jnp.float32
BF16 = jnp.bfloat16

D_MODEL = 1024
N_META = 16
N_HEADS = 16
NOPE = 64
ROPE = 32
QK_DIM = NOPE + ROPE
V_DIM = 64
Q_RANK = 384
KV_RANK = 256
ROPE_THETA = 10000.0
ATTN_W = N_HEADS * V_DIM
SSD_W = 2048
SSD_P = 64
SSD_H = SSD_W // SSD_P
SSD_G = 4
SSD_J = SSD_H // SSD_G
D_STATE = 128
CONV_K = 5
GN = SSD_G * D_STATE
CONV_DIM = SSD_W + 2 * GN
CHUNK = 128
EPS = 1e-6
HP = 128
META_PAD = CHUNK - N_META

C_KV, C_KR, C_Q, C_DT = 0, 256, 384, 768
C_G, C_Z, C_XS, C_MG, C_B, C_C = 1024, 2048, 4096, 6144, 8192, 8704
NP = 9216
O_Q, O_KV, O_KR, O_G, O_Z, O_XBC, O_DT, O_MG = 0, 384, 640, 672, 1696, 3744, 6816, 6880

VMEM_LIMIT = 56 * 1024 * 1024
NEG_INF = float("-inf")


def _cp(sem):
    return pltpu.CompilerParams(dimension_semantics=sem, vmem_limit_bytes=VMEM_LIMIT)


def _sigmoid(x):
    return 1.0 / (1.0 + jnp.exp(-x))


def _rms(x, w):
    return x * lax.rsqrt(jnp.mean(x * x, axis=-1, keepdims=True) + EPS) * w


def _inproj_kernel(x_ref, nw_ref, w_ref, o_ref, dt_ref, u_ref):
    j = pl.program_id(1)

    @pl.when(j == 0)
    def _():
        u_ref[...] = _rms(x_ref[...], nw_ref[...]).astype(BF16)

    acc = jnp.dot(u_ref[...], w_ref[...], preferred_element_type=F32)
    o_ref[...] = acc.astype(BF16)

    @pl.when(j == 0)
    def _():
        dt_ref[...] = acc[:, C_DT:C_DT + 128]


def _inproj(x2, nw, w, tm, tn=1024):
    rows = x2.shape[0]
    return pl.pallas_call(
        _inproj_kernel,
        out_shape=(jax.ShapeDtypeStruct((rows, NP), BF16),
                   jax.ShapeDtypeStruct((rows, 128), F32)),
        grid=(rows // tm, NP // tn),
        in_specs=[pl.BlockSpec((tm, D_MODEL), lambda i, j: (i, 0)),
                  pl.BlockSpec((1, D_MODEL), lambda i, j: (0, 0)),
                  pl.BlockSpec((D_MODEL, tn), lambda i, j: (0, j))],
        out_specs=(pl.BlockSpec((tm, tn), lambda i, j: (i, j)),
                   pl.BlockSpec((tm, 128), lambda i, j: (i, 0))),
        scratch_shapes=[pltpu.VMEM((tm, D_MODEL), BF16)],
        compiler_params=_cp(("parallel", "arbitrary")),
        name="inproj",
    )(x2, nw, w)


def _qkv_kernel(hd_ref, cos_ref, sp_ref, sm_ref, qnw_ref, kvnw_ref, wq_ref, wkv_ref,
                q_ref, k_ref, v_ref, *, scale):
    cos_t, sin_p, sin_m = cos_ref[...], sp_ref[...], sm_ref[...]

    def rope(t):
        return t * cos_t + pltpu.roll(t, 16, 1) * sin_p + pltpu.roll(t, 112, 1) * sin_m

    kv_lat = hd_ref[:, C_KV:C_KV + KV_RANK].astype(F32)
    kvn = _rms(kv_lat, kvnw_ref[...]).astype(BF16)
    kv = jnp.dot(kvn, wkv_ref[...], preferred_element_type=F32)
    kpe = rope(hd_ref[:, C_KR:C_KR + HP].astype(F32))
    for h in range(N_HEADS):
        sl = slice(h * HP, (h + 1) * HP)
        k_ref[:, sl] = (kv[:, sl] + kpe).astype(BF16)
    vv = kv[:, N_HEADS * HP:]
    lane = lax.broadcasted_iota(jnp.int32, vv.shape, 1)
    v_ref[...] = jnp.where(lane % HP == V_DIM, 1.0, vv).astype(BF16)

    q_lat = hd_ref[:, C_Q:C_Q + Q_RANK].astype(F32)
    qn = _rms(q_lat, qnw_ref[...]).astype(BF16)
    q = jnp.dot(qn, wq_ref[...], preferred_element_type=F32)
    for h in range(N_HEADS):
        sl = slice(h * HP, (h + 1) * HP)
        q_ref[:, sl] = (rope(q[:, sl]) * scale).astype(BF16)


def _qkv(proj, tabs, qnw, kvnw, wq, wkv, tm, tiles_per_seq):
    rows = proj.shape[0]
    w = N_HEADS * HP
    tab_spec = pl.BlockSpec((tm, HP), lambda i: (i % tiles_per_seq, 0))
    out = jax.ShapeDtypeStruct((rows, w), BF16)
    return pl.pallas_call(
        functools.partial(_qkv_kernel, scale=QK_DIM ** -0.5),
        out_shape=(out, out, out),
        grid=(rows // tm,),
        in_specs=[pl.BlockSpec((tm, 1024), lambda i: (i, 0)),
                  tab_spec, tab_spec, tab_spec,
                  pl.BlockSpec((1, Q_RANK), lambda i: (0, 0)),
                  pl.BlockSpec((1, KV_RANK), lambda i: (0, 0)),
                  pl.BlockSpec((Q_RANK, w), lambda i: (0, 0)),
                  pl.BlockSpec((KV_RANK, 2 * w), lambda i: (0, 0))],
        out_specs=(pl.BlockSpec((tm, w), lambda i: (i, 0)),) * 3,
        compiler_params=_cp(("parallel",)),
        name="qkv",
    )(proj, *tabs, qnw, kvnw, wq, wkv)


def _flash_kernel(q_ref, k_ref, v_ref, km_ref, vm_ref, o_ref, *, tk, nk):
    nt = (((1,), (1,)), ((), ()))
    outs = []
    for hh in range(2):
        sl = slice(hh * HP, (hh + 1) * HP)
        q = q_ref[:, sl]
        s0 = lax.dot_general(q, km_ref[:, sl], nt, preferred_element_type=F32)
        col = lax.broadcasted_iota(jnp.int32, s0.shape, 1)
        s0 = jnp.where(col >= META_PAD, s0, NEG_INF)
        m = jnp.max(s0, axis=-1, keepdims=True)
        acc = jnp.dot(jnp.exp(s0 - m).astype(BF16), vm_ref[:, sl], preferred_element_type=F32)

        def body(j, carry, sl=sl, q=q):
            m, acc = carry
            off = pl.multiple_of(j * tk, tk)
            s = lax.dot_general(q, k_ref[pl.ds(off, tk), sl], nt, preferred_element_type=F32)
            m_new = jnp.maximum(m, jnp.max(s, axis=-1, keepdims=True))
            alpha = jnp.exp(m - m_new)
            p = jnp.exp(s - m_new).astype(BF16)
            acc = alpha * acc + jnp.dot(p, v_ref[pl.ds(off, tk), sl], preferred_element_type=F32)
            return m_new, acc

        m, acc = lax.fori_loop(0, nk, body, (m, acc))
        outs.append(acc[:, :V_DIM] / acc[:, V_DIM:V_DIM + 1])
    o_ref[...] = jnp.concatenate(outs, axis=-1).astype(BF16)


def _flash(q, k, v, km, vm, tq, tk):
    bsz, seq, _ = q.shape
    return pl.pallas_call(
        functools.partial(_flash_kernel, tk=tk, nk=seq // tk),
        out_shape=jax.ShapeDtypeStruct((bsz, seq, ATTN_W), BF16),
        grid=(bsz, N_HEADS // 2, seq // tq),
        in_specs=[pl.BlockSpec((None, tq, 2 * HP), lambda b, h, i: (b, i, h)),
                  pl.BlockSpec((None, seq, 2 * HP), lambda b, h, i: (b, 0, h)),
                  pl.BlockSpec((None, seq, 2 * HP), lambda b, h, i: (b, 0, h)),
                  pl.BlockSpec((CHUNK, 2 * HP), lambda b, h, i: (0, h)),
                  pl.BlockSpec((CHUNK, 2 * HP), lambda b, h, i: (0, h))],
        out_specs=pl.BlockSpec((None, tq, 2 * V_DIM), lambda b, h, i: (b, i, h)),
        compiler_params=_cp(("parallel", "parallel", "arbitrary")),
        name="flash",
    )(q, k, v, km, vm)


def _conv_taps(ext_ref, w_ref, b_ref, first, rows):
    acc = b_ref[...] + w_ref[0:1, :] * ext_ref[pl.ds(first, rows), :]
    for t in range(1, CONV_K):
        acc = acc + w_ref[t:t + 1, :] * ext_ref[pl.ds(first + t, rows), :]
    return acc * _sigmoid(acc)


def _conv_kernel(main_ref, prev_ref, next_ref, meta_ref, w_ref, b_ref, o_ref, ext_ref, *, tr, nr):
    r = pl.program_id(2)
    prev = jnp.where(r == 0, meta_ref[CHUNK - 16:CHUNK, :], prev_ref[...])
    nxt = jnp.where(r == nr - 1, jnp.zeros_like(next_ref[...]), next_ref[...])
    ext_ref[0:16, :] = prev.astype(F32)
    ext_ref[16:16 + tr, :] = main_ref[...].astype(F32)
    ext_ref[16 + tr:32 + tr, :] = nxt.astype(F32)
    o_ref[...] = _conv_taps(ext_ref, w_ref, b_ref, 16 - CONV_K // 2, tr).astype(BF16)


def _conv_col(c):
    return jnp.where(c < 4, C_XS // 512 + c, C_B // 512 + (c - 4))


def _conv(proj3, proj_meta, cw, cb, tr):
    bsz, seq, _ = proj3.shape
    nr = seq // tr
    hb = tr // 16
    return pl.pallas_call(
        functools.partial(_conv_kernel, tr=tr, nr=nr),
        out_shape=jax.ShapeDtypeStruct((bsz, seq, CONV_DIM), BF16),
        grid=(bsz, CONV_DIM // 512, nr),
        in_specs=[pl.BlockSpec((None, tr, 512), lambda b, c, r: (b, r, _conv_col(c))),
                  pl.BlockSpec((None, 16, 512),
                               lambda b, c, r: (b, jnp.maximum(r * hb - 1, 0), _conv_col(c))),
                  pl.BlockSpec((None, 16, 512),
                               lambda b, c, r: (b, jnp.minimum((r + 1) * hb, seq // 16 - 1), _conv_col(c))),
                  pl.BlockSpec((CHUNK, 512), lambda b, c, r: (0, _conv_col(c))),
                  pl.BlockSpec((CONV_K, 512), lambda b, c, r: (0, c)),
                  pl.BlockSpec((1, 512), lambda b, c, r: (0, c))],
        out_specs=pl.BlockSpec((None, tr, 512), lambda b, c, r: (b, r, c)),
        scratch_shapes=[pltpu.VMEM((tr + 32, 512), F32)],
        compiler_params=_cp(("parallel", "parallel", "arbitrary")),
        name="conv",
    )(proj3, proj3, proj3, proj_meta, cw, cb)


def _conv_meta_kernel(meta_ref, main_ref, w_ref, b_ref, o_ref, ext_ref):
    ext_ref[0:32, :] = meta_ref[CHUNK - 32:CHUNK, :].astype(F32)
    ext_ref[32:48, :] = main_ref[...].astype(F32)
    o_ref[0:META_PAD, :] = jnp.zeros((META_PAD, 512), BF16)
    o_ref[META_PAD:CHUNK, :] = _conv_taps(ext_ref, w_ref, b_ref, 16 - CONV_K // 2, N_META).astype(BF16)


def _conv_meta(proj3, proj_meta, cw, cb):
    bsz = proj3.shape[0]
    return pl.pallas_call(
        _conv_meta_kernel,
        out_shape=jax.ShapeDtypeStruct((bsz, CHUNK, CONV_DIM), BF16),
        grid=(bsz, CONV_DIM // 512),
        in_specs=[pl.BlockSpec((CHUNK, 512), lambda b, c: (0, _conv_col(c))),
                  pl.BlockSpec((None, 16, 512), lambda b, c: (b, 0, _conv_col(c))),
                  pl.BlockSpec((CONV_K, 512), lambda b, c: (0, c)),
                  pl.BlockSpec((1, 512), lambda b, c: (0, c))],
        out_specs=pl.BlockSpec((None, CHUNK, 512), lambda b, c: (b, 0, c)),
        scratch_shapes=[pltpu.VMEM((48, 512), F32)],
        compiler_params=_cp(("parallel", "parallel")),
        name="conv_meta",
    )(proj_meta, proj3, cw, cb)


def _dt_kernel(raw_ref, bias_ref, alog_ref, dt_ref, cc_ref, cr_ref, *, cpb, valid_from):
    row = lax.broadcasted_iota(jnp.int32, (CHUNK, 128), 0)
    col = lax.broadcasted_iota(jnp.int32, (CHUNK, 128), 1)
    tri = (col <= row).astype(F32)
    triu = (col >= row).astype(F32)
    a = -jnp.exp(alog_ref[...])
    for c in range(cpb):
        rs = slice(c * CHUNK, (c + 1) * CHUNK)
        x = raw_ref[rs, :] + bias_ref[...]
        dt = jnp.maximum(x, 0.0) + jnp.log(1.0 + jnp.exp(-jnp.abs(x)))
        dt = jnp.where(col < 2 * SSD_H, dt, 0.0)
        if valid_from:
            dt = jnp.where(row >= valid_from, dt, 0.0)
        da = dt * a
        cf = jnp.dot(tri, da, precision=lax.Precision.HIGHEST, preferred_element_type=F32)
        cb = jnp.dot(triu, da, precision=lax.Precision.HIGHEST, preferred_element_type=F32)
        cum = jnp.where(col < SSD_H, cf, cb)
        dt_ref[rs, :] = dt
        cc_ref[rs, :] = cum
        cr_ref[c] = cum.T


def _dtprep(raw, bias, alog, cpb, valid_from):
    rows = raw.shape[0]
    nch = rows // CHUNK
    blk = cpb * CHUNK
    o = jax.ShapeDtypeStruct((rows, 128), F32)
    return pl.pallas_call(
        functools.partial(_dt_kernel, cpb=cpb, valid_from=valid_from),
        out_shape=(o, o, jax.ShapeDtypeStruct((nch, 128, CHUNK), F32)),
        grid=(nch // cpb,),
        in_specs=[pl.BlockSpec((blk, 128), lambda i: (i, 0)),
                  pl.BlockSpec((1, 128), lambda i: (0, 0)),
                  pl.BlockSpec((1, 128), lambda i: (0, 0))],
        out_specs=(pl.BlockSpec((blk, 128), lambda i: (i, 0)),
                   pl.BlockSpec((blk, 128), lambda i: (i, 0)),
                   pl.BlockSpec((cpb, 128, CHUNK), lambda i: (i, 0, 0))),
        compiler_params=_cp(("parallel",)),
        name="dtprep",
    )(raw, bias, alog)


def _ssd_chunk(state, xbc, dtc, cc, cr, g, fwd, want_y):
    row = lax.broadcasted_iota(jnp.int32, (CHUNK, CHUNK), 0)
    col = lax.broadcasted_iota(jnp.int32, (CHUNK, CHUNK), 1)
    keep = (col <= row) if fwd else (col >= row)
    lo = col < SSD_P
    hb = (0 if fwd else SSD_H) + g * SSD_J
    edge = CHUNK - 1 if fwd else 0

    xs = xbc[:, g * 512:(g + 1) * 512].astype(F32)
    bm = xbc[:, SSD_W + g * D_STATE:SSD_W + (g + 1) * D_STATE]
    cm = xbc[:, SSD_W + GN + g * D_STATE:SSD_W + GN + (g + 1) * D_STATE]
    bt = bm.astype(F32).T.astype(BF16)
    if want_y:
        cbm = lax.dot_general(cm, bm, (((1,), (1,)), ((), ())), preferred_element_type=F32)
        yoff = jnp.dot(cm, state.astype(BF16), preferred_element_type=F32)

    def bcl(a, j):
        return jnp.broadcast_to(a[:, hb + j:hb + j + 1], (CHUNK, CHUNK))

    ys, xws, decs = [], [], []
    for kk in range(SSD_J // 2):
        j0, j1 = 2 * kk, 2 * kk + 1
        cc0, cc1 = bcl(cc, j0), bcl(cc, j1)
        cc_pair = jnp.where(lo, cc0, cc1)
        dt_pair = jnp.where(lo, bcl(dtc, j0), bcl(dtc, j1))
        xdt = xs[:, kk * 128:(kk + 1) * 128] * dt_pair
        tot = cc_pair[edge:edge + 1, :]
        xws.append((xdt * jnp.exp(tot - cc_pair)).astype(BF16))
        decs.append(jnp.exp(tot))
        if want_y:
            xdt_b = xdt.astype(BF16)

            def ydiag(ccj, j):
                crj = jnp.broadcast_to(cr[hb + j:hb + j + 1, :], (CHUNK, CHUNK))
                decay = jnp.exp(jnp.where(keep, ccj - crj, NEG_INF))
                return jnp.dot((cbm * decay).astype(BF16), xdt_b, preferred_element_type=F32)

            yd = jnp.where(lo, ydiag(cc0, j0), ydiag(cc1, j1))
            ys.append(yd + yoff[:, kk * 128:(kk + 1) * 128] * jnp.exp(cc_pair))
    xw = jnp.concatenate(xws, axis=1)
    dec = jnp.concatenate(decs, axis=1)
    new_state = state * dec + jnp.dot(bt, xw, preferred_element_type=F32)
    return new_state, (jnp.concatenate(ys, axis=1) if want_y else None)


def _ssd_bwd_kernel(xbc_ref, dt_ref, cc_ref, cr_ref, yb_ref, st_ref, *, ncs):
    s = pl.program_id(1)

    @pl.when(s == 0)
    def _():
        st_ref[...] = jnp.zeros_like(st_ref)

    for g in range(SSD_G):
        def body(i, carry, g=g):
            c = ncs - 1 - i
            r0 = pl.multiple_of(c * CHUNK, CHUNK)
            rs = pl.ds(r0, CHUNK)
            st, y = _ssd_chunk(st_ref[g], xbc_ref[rs, :], dt_ref[rs, :], cc_ref[rs, :], cr_ref[c],
                               g, False, True)
            st_ref[g] = st
            yb_ref[rs, g * 512:(g + 1) * 512] = y.astype(BF16)
            return carry

        lax.fori_loop(0, ncs, body, 0)


def _ssd_fwd_kernel(xbc_ref, dt_ref, cc_ref, cr_ref, z_ref, yb_ref, xm_ref, dtm_ref, ccm_ref, crm_ref,
                    dsk_ref, nw_ref, o_ref, st_ref, *, ncs):
    s = pl.program_id(1)

    @pl.when(s == 0)
    def _():
        for g in range(SSD_G):
            st, _ = _ssd_chunk(jnp.zeros((D_STATE, 512), F32), xm_ref[...], dtm_ref[...], ccm_ref[...],
                               crm_ref[0], g, True, False)
            st_ref[g] = st

    for g in range(SSD_G):
        gs = slice(g * 512, (g + 1) * 512)

        def body(c, carry, g=g, gs=gs):
            r0 = pl.multiple_of(c * CHUNK, CHUNK)
            rs = pl.ds(r0, CHUNK)
            xbc = xbc_ref[rs, :]
            st, y = _ssd_chunk(st_ref[g], xbc, dt_ref[rs, :], cc_ref[rs, :], cr_ref[c], g, True, True)
            st_ref[g] = st
            y = y + yb_ref[rs, gs].astype(F32) + dsk_ref[:, gs] * xbc[:, gs].astype(F32)
            z = z_ref[rs, gs].astype(F32)
            y = y * (z * _sigmoid(z))
            o_ref[rs, gs] = _rms(y, nw_ref[:, gs]).astype(BF16)
            return carry

        lax.fori_loop(0, ncs, body, 0)


def _ssd(xbc, dt, cc, cr, proj3, xbc_m, dt_m, cc_m, cr_m, dsk, nw, ts):
    bsz, seq, _ = xbc.shape
    ncs = ts // CHUNK
    nseg = seq // ts
    dt3 = dt.reshape(bsz, seq, 128)
    cc3 = cc.reshape(bsz, seq, 128)
    cr4 = cr.reshape(bsz, seq // CHUNK, 128, CHUNK)
    st = pltpu.VMEM((SSD_G, D_STATE, 512), F32)

    def seg_specs(rev):
        ix = (lambda s: nseg - 1 - s) if rev else (lambda s: s)
        return [pl.BlockSpec((None, ts, CONV_DIM), lambda b, s: (b, ix(s), 0)),
                pl.BlockSpec((None, ts, 128), lambda b, s: (b, ix(s), 0)),
                pl.BlockSpec((None, ts, 128), lambda b, s: (b, ix(s), 0)),
                pl.BlockSpec((None, ncs, 128, CHUNK), lambda b, s: (b, ix(s), 0, 0))]

    yb = pl.pallas_call(
        functools.partial(_ssd_bwd_kernel, ncs=ncs),
        out_shape=jax.ShapeDtypeStruct((bsz, seq, SSD_W), BF16),
        grid=(bsz, nseg),
        in_specs=seg_specs(True),
        out_specs=pl.BlockSpec((None, ts, SSD_W), lambda b, s: (b, nseg - 1 - s, 0)),
        scratch_shapes=[st],
        compiler_params=_cp(("parallel", "arbitrary")),
        name="ssd_bwd",
    )(xbc, dt3, cc3, cr4)

    return pl.pallas_call(
        functools.partial(_ssd_fwd_kernel, ncs=ncs),
        out_shape=jax.ShapeDtypeStruct((bsz, seq, SSD_W), BF16),
        grid=(bsz, nseg),
        in_specs=seg_specs(False) + [
            pl.BlockSpec((None, ts, SSD_W), lambda b, s: (b, s, C_Z // SSD_W)),
            pl.BlockSpec((None, ts, SSD_W), lambda b, s: (b, s, 0)),
            pl.BlockSpec((None, CHUNK, CONV_DIM), lambda b, s: (b, 0, 0)),
            pl.BlockSpec((CHUNK, 128), lambda b, s: (0, 0)),
            pl.BlockSpec((CHUNK, 128), lambda b, s: (0, 0)),
            pl.BlockSpec((1, 128, CHUNK), lambda b, s: (0, 0, 0)),
            pl.BlockSpec((1, SSD_W), lambda b, s: (0, 0)),
            pl.BlockSpec((1, SSD_W), lambda b, s: (0, 0))],
        out_specs=pl.BlockSpec((None, ts, SSD_W), lambda b, s: (b, s, 0)),
        scratch_shapes=[st],
        compiler_params=_cp(("parallel", "arbitrary")),
        name="ssd_fwd",
    )(xbc, dt3, cc3, cr4, proj3, yb, xbc_m, dt_m, cc_m, cr_m, dsk, nw)


def _out_kernel(o_ref, g_ref, ys_ref, mg_ref, x_ref, wa_ref, ws_ref, wo_ref, fnw_ref, out_ref):
    g = g_ref[...].astype(F32)
    ag = (o_ref[...].astype(F32) * (g * _sigmoid(g))).astype(BF16)
    ya = jnp.dot(ag, wa_ref[...], preferred_element_type=F32)
    ys = jnp.dot(ys_ref[...], ws_ref[...], preferred_element_type=F32)
    gate = _sigmoid(mg_ref[...].astype(F32))
    mixed = (gate[:, :D_MODEL] * ya + gate[:, D_MODEL:] * ys).astype(BF16)
    h = x_ref[...] + jnp.dot(mixed, wo_ref[...], preferred_element_type=F32)
    out_ref[...] = _rms(h, fnw_ref[...])


def _outproj(o2, proj, ys2, x2, wa, ws, wo, fnw, tm):
    rows = x2.shape[0]
    full = lambda shape: pl.BlockSpec(shape, lambda i: (0, 0))
    return pl.pallas_call(
        _out_kernel,
        out_shape=jax.ShapeDtypeStruct((rows, D_MODEL), F32),
        grid=(rows // tm,),
        in_specs=[pl.BlockSpec((tm, ATTN_W), lambda i: (i, 0)),
                  pl.BlockSpec((tm, ATTN_W), lambda i: (i, C_G // ATTN_W)),
                  pl.BlockSpec((tm, SSD_W), lambda i: (i, 0)),
                  pl.BlockSpec((tm, 2 * D_MODEL), lambda i: (i, C_MG // (2 * D_MODEL))),
                  pl.BlockSpec((tm, D_MODEL), lambda i: (i, 0)),
                  full((ATTN_W, D_MODEL)), full((SSD_W, D_MODEL)), full((D_MODEL, D_MODEL)),
                  full((1, D_MODEL))],
        out_specs=pl.BlockSpec((tm, D_MODEL), lambda i: (i, 0)),
        compiler_params=_cp(("parallel",)),
        name="outproj",
    )(o2, proj, ys2, proj, x2, wa, ws, wo, fnw)


def _rope_tables(pos):
    inv_freq = ROPE_THETA ** (-jnp.arange(0, ROPE, 2, dtype=F32) / ROPE)
    ang = pos.astype(F32)[:, None] * inv_freq[None, :]
    cos, sin = jnp.cos(ang), jnp.sin(ang)
    n = pos.shape[0]
    one = jnp.ones((n, NOPE), F32)
    z16 = jnp.zeros((n, ROPE // 2), F32)
    z32 = jnp.zeros((n, HP - QK_DIM), F32)
    z64 = jnp.zeros((n, NOPE), F32)
    cos_t = jnp.concatenate([one, cos, cos, z32 + 1.0], axis=1)
    sin_p = jnp.concatenate([z64, z16, sin, z32], axis=1)
    sin_m = jnp.concatenate([z64, -sin, z16, z32], axis=1)
    return cos_t, sin_p, sin_m


def _pick(n, prefs):
    for t in prefs:
        if n % t == 0:
            return t
    raise ValueError(f"no tile for {n}")


def kernel(x, meta_tokens, norm_w, w_in, q_norm_w, w_uq, kv_norm_w, w_ukv, w_attn_proj, conv_w, conv_b,
           dt_bias, a_log, d_skip, ssd_norm_w, w_ssd_proj, w_out, final_norm_w):
    bsz, seq, d = x.shape
    assert d == D_MODEL and seq % 128 == 0 and norm_w.shape[0] == 1
    rows = bsz * seq
    zc = lambda n: jnp.zeros((D_MODEL, n), F32)
    w = w_in[0]
    w_p = jnp.concatenate([
        w[:, O_KV:O_KV + KV_RANK], zc(NOPE), w[:, O_KR:O_KR + ROPE], zc(HP - QK_DIM),
        w[:, O_Q:O_Q + Q_RANK], w[:, O_DT:O_DT + 2 * SSD_H], zc(128 - 2 * SSD_H), zc(128),
        w[:, O_G:O_G + ATTN_W], w[:, O_Z:O_Z + SSD_W], w[:, O_XBC:O_XBC + SSD_W],
        w[:, O_MG:O_MG + 2 * D_MODEL], w[:, O_XBC + SSD_W:O_XBC + CONV_DIM]], axis=1).astype(BF16)
    assert w_p.shape[1] == NP
    pad_h = lambda t: jnp.pad(t, ((0, 0), (0, 0), (0, HP - t.shape[-1]))).reshape(t.shape[0], N_HEADS * HP)
    wq = pad_h(w_uq[0].reshape(Q_RANK, N_HEADS, QK_DIM)).astype(BF16)
    ukv = w_ukv[0].reshape(KV_RANK, N_HEADS, NOPE + V_DIM)
    wkv = jnp.concatenate([pad_h(ukv[..., :NOPE]), pad_h(ukv[..., NOPE:])], axis=1).astype(BF16)
    row = lambda t: t.reshape(1, -1)
    pad128 = lambda t: jnp.pad(t.reshape(1, -1), ((0, 0), (0, 128 - t.size)))

    tm = _pick(rows, (1024, 512, 256, 128))
    x2 = x.reshape(rows, D_MODEL)
    proj, dt_raw = _inproj(x2, row(norm_w[0]), w_p, tm)
    xm = jnp.concatenate([jnp.zeros((META_PAD, D_MODEL), F32), meta_tokens.astype(F32)], axis=0)
    proj_m, dt_raw_m = _inproj(xm, row(norm_w[0]), w_p, CHUNK)

    tabs = _rope_tables(N_META + jnp.arange(seq))
    tabs_m = _rope_tables(jnp.maximum(jnp.arange(CHUNK) - META_PAD, 0))
    tq2 = _pick(seq, (512, 256, 128))
    q, k, v = _qkv(proj, tabs, row(q_norm_w[0]), row(kv_norm_w[0]), wq, wkv, tq2, seq // tq2)
    _, km, vm = _qkv(proj_m, tabs_m, row(q_norm_w[0]), row(kv_norm_w[0]), wq, wkv, CHUNK, 1)

    r3 = lambda t: t.reshape(bsz, seq, t.shape[-1])
    tq = _pick(seq, (512, 256, 128))
    tk = _pick(seq, (512, 256, 128))
    o = _flash(r3(q), r3(k), r3(v), km, vm, tq, tk)

    proj3 = r3(proj)
    cw, cb = conv_w[0], row(conv_b[0])
    xbc = _conv(proj3, proj_m, cw, cb, _pick(seq, (1024, 512, 256, 128)))
    xbc_m = _conv_meta(proj3, proj_m, cw, cb)
    bias, alog = pad128(dt_bias[0]), pad128(a_log[0])
    dt, cc, cr = _dtprep(dt_raw, bias, alog, _pick(rows // CHUNK, (8, 4, 2, 1)), 0)
    dt_m, cc_m, cr_m = _dtprep(dt_raw_m, bias, alog, 1, META_PAD)
    dsk = row(jnp.repeat(d_skip[0], SSD_P))
    ys = _ssd(xbc, dt, cc, cr, proj3, xbc_m, dt_m, cc_m, cr_m, dsk, row(ssd_norm_w[0]),
              _pick(seq, (512, 256, 128)))

    out = _outproj(o.reshape(rows, ATTN_W), proj, ys.reshape(rows, SSD_W), x2,
                   w_attn_proj[0].astype(BF16), w_ssd_proj[0].astype(BF16), w_out[0].astype(BF16),
                   row(final_norm_w), _pick(rows, (512, 256, 128)))
    return out.reshape(bsz, seq, D_MODEL)
```

```python
import functools
import math

import jax
import jax.numpy as jnp
from jax import lax
from jax.experimental import pallas as pl
from jax.experimental.pallas import tpu as pltpu

F32 = jnp.float32
BF16 = jnp.bfloat16

D_MODEL = 1024
N_META = 16
N_HEADS = 16
NOPE = 64
ROPE = 32
QK_DIM = NOPE + ROPE
V_DIM = 64
Q_RANK = 384
KV_RANK = 256
ROPE_THETA = 10000.0
ATTN_W = N_HEADS * V_DIM
SSD_W = 2048
SSD_P = 64
SSD_H = SSD_W // SSD_P
SSD_G = 4
SSD_J = SSD_H // SSD_G
D_STATE = 128
CONV_K = 5
GN = SSD_G * D_STATE
CONV_DIM = SSD_W + 2 * GN
CHUNK = 128
EPS = 1e-6
HP = 128
META_PAD = CHUNK - N_META

C_KV, C_KR, C_Q, C_DT = 0, 256, 384, 768
C_G, C_Z, C_XS, C_MG, C_B, C_C = 1024, 2048, 4096, 6144, 8192, 8704
NP = 9216
O_Q, O_KV, O_KR, O_G, O_Z, O_XBC, O_DT, O_MG = 0, 384, 640, 672, 1696, 3744, 6816, 6880

VMEM_LIMIT = 56 * 1024 * 1024
NEG_INF = float("-inf")


def _cp(sem):
    return pltpu.CompilerParams(dimension_semantics=sem, vmem_limit_bytes=VMEM_LIMIT)


def _sigmoid(x):
    return 1.0 / (1.0 + jnp.exp(-x))


def _rms(x, w):
    return x * lax.rsqrt(jnp.mean(x * x, axis=-1, keepdims=True) + EPS) * w


def _inproj_kernel(x_ref, nw_ref, w_ref, o_ref, dt_ref, u_ref):
    j = pl.program_id(1)

    @pl.when(j == 0)
    def _():
        u_ref[...] = _rms(x_ref[...], nw_ref[...]).astype(BF16)

    acc = jnp.dot(u_ref[...], w_ref[...], preferred_element_type=F32)
    o_ref[...] = acc.astype(BF16)

    @pl.when(j == 0)
    def _():
        dt_ref[...] = acc[:, C_DT:C_DT + 128]


def _inproj(x2, nw, w, tm, tn=1024):
    rows = x2.shape[0]
    return pl.pallas_call(
        _inproj_kernel,
        out_shape=(jax.ShapeDtypeStruct((rows, NP), BF16),
                   jax.ShapeDtypeStruct((rows, 128), F32)),
        grid=(rows // tm, NP // tn),
        in_specs=[pl.BlockSpec((tm, D_MODEL), lambda i, j: (i, 0)),
                  pl.BlockSpec((1, D_MODEL), lambda i, j: (0, 0)),
                  pl.BlockSpec((D_MODEL, tn), lambda i, j: (0, j))],
        out_specs=(pl.BlockSpec((tm, tn), lambda i, j: (i, j)),
                   pl.BlockSpec((tm, 128), lambda i, j: (i, 0))),
        scratch_shapes=[pltpu.VMEM((tm, D_MODEL), BF16)],
        compiler_params=_cp(("parallel", "arbitrary")),
        name="inproj",
    )(x2, nw, w)


def _qkv_kernel(hd_ref, cos_ref, sp_ref, sm_ref, qnw_ref, kvnw_ref, wq_ref, wkv_ref,
                q_ref, k_ref, v_ref, *, scale):
    cos_t, sin_p, sin_m = cos_ref[...], sp_ref[...], sm_ref[...]

    def rope(t):
        return t * cos_t + pltpu.roll(t, 16, 1) * sin_p + pltpu.roll(t, 112, 1) * sin_m

    kv_lat = hd_ref[:, C_KV:C_KV + KV_RANK].astype(F32)
    kvn = _rms(kv_lat, kvnw_ref[...]).astype(BF16)
    kv = jnp.dot(kvn, wkv_ref[...], preferred_element_type=F32)
    kpe = rope(hd_ref[:, C_KR:C_KR + HP].astype(F32))
    for h in range(N_HEADS):
        sl = slice(h * HP, (h + 1) * HP)
        k_ref[:, sl] = (kv[:, sl] + kpe).astype(BF16)
    vv = kv[:, N_HEADS * HP:]
    lane = lax.broadcasted_iota(jnp.int32, vv.shape, 1)
    v_ref[...] = jnp.where(lane % HP == V_DIM, 1.0, vv).astype(BF16)

    q_lat = hd_ref[:, C_Q:C_Q + Q_RANK].astype(F32)
    qn = _rms(q_lat, qnw_ref[...]).astype(BF16)
    q = jnp.dot(qn, wq_ref[...], preferred_element_type=F32)
    for h in range(N_HEADS):
        sl = slice(h * HP, (h + 1) * HP)
        q_ref[:, sl] = (rope(q[:, sl]) * scale).astype(BF16)


def _qkv(proj, tabs, qnw, kvnw, wq, wkv, tm, tiles_per_seq):
    rows = proj.shape[0]
    w = N_HEADS * HP
    tab_spec = pl.BlockSpec((tm, HP), lambda i: (i % tiles_per_seq, 0))
    out = jax.ShapeDtypeStruct((rows, w), BF16)
    return pl.pallas_call(
        functools.partial(_qkv_kernel, scale=math.log2(math.e) * QK_DIM ** -0.5),
        out_shape=(out, out, out),
        grid=(rows // tm,),
        in_specs=[pl.BlockSpec((tm, 1024), lambda i: (i, 0)),
                  tab_spec, tab_spec, tab_spec,
                  pl.BlockSpec((1, Q_RANK), lambda i: (0, 0)),
                  pl.BlockSpec((1, KV_RANK), lambda i: (0, 0)),
                  pl.BlockSpec((Q_RANK, w), lambda i: (0, 0)),
                  pl.BlockSpec((KV_RANK, 2 * w), lambda i: (0, 0))],
        out_specs=(pl.BlockSpec((tm, w), lambda i: (i, 0)),) * 3,
        compiler_params=_cp(("parallel",)),
        name="qkv",
    )(proj, *tabs, qnw, kvnw, wq, wkv)


def _flash_kernel(q_ref, k_ref, v_ref, km_ref, vm_ref, o_ref, m_ref, acc_ref, *, tk, nk):
    nt = (((1,), (1,)), ((), ()))
    heads = [slice(hh * HP, (hh + 1) * HP) for hh in range(2)]
    for hh, sl in enumerate(heads):
        s0 = lax.dot_general(q_ref[:, sl], km_ref[:, sl], nt, preferred_element_type=F32)
        col = lax.broadcasted_iota(jnp.int32, s0.shape, 1)
        s0 = jnp.where(col >= META_PAD, s0, NEG_INF)
        m = jnp.broadcast_to(jnp.max(s0, axis=-1, keepdims=True), s0.shape)
        m_ref[hh] = m
        acc_ref[hh] = jnp.dot(jnp.exp2(s0 - m).astype(BF16), vm_ref[:, sl], preferred_element_type=F32)

    def body(j, carry):
        off = pl.multiple_of(j * tk, tk)
        for hh, sl in enumerate(heads):
            s = lax.dot_general(q_ref[:, sl], k_ref[pl.ds(off, tk), sl], nt, preferred_element_type=F32)
            blocks = [s[:, c * HP:(c + 1) * HP] for c in range(tk // HP)]
            m = m_ref[hh]
            m_new = jnp.maximum(m, jnp.max(functools.reduce(jnp.maximum, blocks), axis=-1, keepdims=True))
            p = jnp.concatenate([jnp.exp2(blk - m_new) for blk in blocks], axis=-1).astype(BF16)
            pv = jnp.dot(p, v_ref[pl.ds(off, tk), sl], preferred_element_type=F32)
            acc_ref[hh] = jnp.exp2(m - m_new) * acc_ref[hh] + pv
            m_ref[hh] = m_new
        return carry

    lax.fori_loop(0, nk, body, 0, unroll=2)
    outs = [acc_ref[hh][:, :V_DIM] / acc_ref[hh][:, V_DIM:V_DIM + 1] for hh in range(2)]
    o_ref[...] = jnp.concatenate(outs, axis=-1).astype(BF16)


def _flash(q, k, v, km, vm, tq, tk):
    bsz, seq, _ = q.shape
    return pl.pallas_call(
        functools.partial(_flash_kernel, tk=tk, nk=seq // tk),
        out_shape=jax.ShapeDtypeStruct((bsz, seq, ATTN_W), BF16),
        grid=(bsz, N_HEADS // 2, seq // tq),
        in_specs=[pl.BlockSpec((None, tq, 2 * HP), lambda b, h, i: (b, i, h)),
                  pl.BlockSpec((None, seq, 2 * HP), lambda b, h, i: (b, 0, h)),
                  pl.BlockSpec((None, seq, 2 * HP), lambda b, h, i: (b, 0, h)),
                  pl.BlockSpec((CHUNK, 2 * HP), lambda b, h, i: (0, h)),
                  pl.BlockSpec((CHUNK, 2 * HP), lambda b, h, i: (0, h))],
        out_specs=pl.BlockSpec((None, tq, 2 * V_DIM), lambda b, h, i: (b, i, h)),
        scratch_shapes=[pltpu.VMEM((2, tq, HP), F32), pltpu.VMEM((2, tq, HP), F32)],
        compiler_params=_cp(("parallel", "parallel", "arbitrary")),
        name="flash",
    )(q, k, v, km, vm)


def _conv_taps(ext_ref, w_ref, b_ref, first, rows):
    acc = b_ref[...] + w_ref[0:1, :] * ext_ref[pl.ds(first, rows), :]
    for t in range(1, CONV_K):
        acc = acc + w_ref[t:t + 1, :] * ext_ref[pl.ds(first + t, rows), :]
    return acc * _sigmoid(acc)


def _conv_kernel(main_ref, prev_ref, next_ref, meta_ref, w_ref, b_ref, o_ref, ext_ref, *, tr, nr):
    r = pl.program_id(2)
    prev = jnp.where(r == 0, meta_ref[CHUNK - 16:CHUNK, :], prev_ref[...])
    nxt = jnp.where(r == nr - 1, jnp.zeros_like(next_ref[...]), next_ref[...])
    ext_ref[0:16, :] = prev.astype(F32)
    ext_ref[16:16 + tr, :] = main_ref[...].astype(F32)
    ext_ref[16 + tr:32 + tr, :] = nxt.astype(F32)
    o_ref[...] = _conv_taps(ext_ref, w_ref, b_ref, 16 - CONV_K // 2, tr).astype(BF16)


def _conv_col(c):
    return jnp.where(c < 4, C_XS // 512 + c, C_B // 512 + (c - 4))


def _conv(proj3, proj_meta, cw, cb, tr):
    bsz, seq, _ = proj3.shape
    nr = seq // tr
    hb = tr // 16
    return pl.pallas_call(
        functools.partial(_conv_kernel, tr=tr, nr=nr),
        out_shape=jax.ShapeDtypeStruct((bsz, seq, CONV_DIM), BF16),
        grid=(bsz, CONV_DIM // 512, nr),
        in_specs=[pl.BlockSpec((None, tr, 512), lambda b, c, r: (b, r, _conv_col(c))),
                  pl.BlockSpec((None, 16, 512),
                               lambda b, c, r: (b, jnp.maximum(r * hb - 1, 0), _conv_col(c))),
                  pl.BlockSpec((None, 16, 512),
                               lambda b, c, r: (b, jnp.minimum((r + 1) * hb, seq // 16 - 1), _conv_col(c))),
                  pl.BlockSpec((CHUNK, 512), lambda b, c, r: (0, _conv_col(c))),
                  pl.BlockSpec((CONV_K, 512), lambda b, c, r: (0, c)),
                  pl.BlockSpec((1, 512), lambda b, c, r: (0, c))],
        out_specs=pl.BlockSpec((None, tr, 512), lambda b, c, r: (b, r, c)),
        scratch_shapes=[pltpu.VMEM((tr + 32, 512), F32)],
        compiler_params=_cp(("parallel", "parallel", "arbitrary")),
        name="conv",
    )(proj3, proj3, proj3, proj_meta, cw, cb)


def _conv_meta_kernel(meta_ref, main_ref, w_ref, b_ref, o_ref, ext_ref):
    ext_ref[0:32, :] = meta_ref[CHUNK - 32:CHUNK, :].astype(F32)
    ext_ref[32:48, :] = main_ref[...].astype(F32)
    o_ref[0:META_PAD, :] = jnp.zeros((META_PAD, 512), BF16)
    o_ref[META_PAD:CHUNK, :] = _conv_taps(ext_ref, w_ref, b_ref, 16 - CONV_K // 2, N_META).astype(BF16)


def _conv_meta(proj3, proj_meta, cw, cb):
    bsz = proj3.shape[0]
    return pl.pallas_call(
        _conv_meta_kernel,
        out_shape=jax.ShapeDtypeStruct((bsz, CHUNK, CONV_DIM), BF16),
        grid=(bsz, CONV_DIM // 512),
        in_specs=[pl.BlockSpec((CHUNK, 512), lambda b, c: (0, _conv_col(c))),
                  pl.BlockSpec((None, 16, 512), lambda b, c: (b, 0, _conv_col(c))),
                  pl.BlockSpec((CONV_K, 512), lambda b, c: (0, c)),
                  pl.BlockSpec((1, 512), lambda b, c: (0, c))],
        out_specs=pl.BlockSpec((None, CHUNK, 512), lambda b, c: (b, 0, c)),
        scratch_shapes=[pltpu.VMEM((48, 512), F32)],
        compiler_params=_cp(("parallel", "parallel")),
        name="conv_meta",
    )(proj_meta, proj3, cw, cb)


def _dt_kernel(raw_ref, bias_ref, alog_ref, dt_ref, cc_ref, cr_ref, *, cpb, valid_from):
    row = lax.broadcasted_iota(jnp.int32, (CHUNK, 128), 0)
    col = lax.broadcasted_iota(jnp.int32, (CHUNK, 128), 1)
    tri = (col <= row).astype(F32)
    triu = (col >= row).astype(F32)
    a = -jnp.exp(alog_ref[...])
    for c in range(cpb):
        rs = slice(c * CHUNK, (c + 1) * CHUNK)
        x = raw_ref[rs, :] + bias_ref[...]
        dt = jnp.maximum(x, 0.0) + jnp.log(1.0 + jnp.exp(-jnp.abs(x)))
        dt = jnp.where(col < 2 * SSD_H, dt, 0.0)
        if valid_from:
            dt = jnp.where(row >= valid_from, dt, 0.0)
        da = dt * a
        cf = jnp.dot(tri, da, precision=lax.Precision.HIGHEST, preferred_element_type=F32)
        cb = jnp.dot(triu, da, precision=lax.Precision.HIGHEST, preferred_element_type=F32)
        cum = jnp.where(col < SSD_H, cf, cb)
        dt_ref[rs, :] = dt
        cc_ref[rs, :] = cum
        cr_ref[c] = cum.T


def _dtprep(raw, bias, alog, cpb, valid_from):
    rows = raw.shape[0]
    nch = rows // CHUNK
    blk = cpb * CHUNK
    o = jax.ShapeDtypeStruct((rows, 128), F32)
    return pl.pallas_call(
        functools.partial(_dt_kernel, cpb=cpb, valid_from=valid_from),
        out_shape=(o, o, jax.ShapeDtypeStruct((nch, 128, CHUNK), F32)),
        grid=(nch // cpb,),
        in_specs=[pl.BlockSpec((blk, 128), lambda i: (i, 0)),
                  pl.BlockSpec((1, 128), lambda i: (0, 0)),
                  pl.BlockSpec((1, 128), lambda i: (0, 0))],
        out_specs=(pl.BlockSpec((blk, 128), lambda i: (i, 0)),
                   pl.BlockSpec((blk, 128), lambda i: (i, 0)),
                   pl.BlockSpec((cpb, 128, CHUNK), lambda i: (i, 0, 0))),
        compiler_params=_cp(("parallel",)),
        name="dtprep",
    )(raw, bias, alog)


def _ssd_chunk(state, xbc, dtc, cc, cr, g, fwd, want_y):
    row = lax.broadcasted_iota(jnp.int32, (CHUNK, CHUNK), 0)
    col = lax.broadcasted_iota(jnp.int32, (CHUNK, CHUNK), 1)
    keep = (col <= row) if fwd else (col >= row)
    lo = col < SSD_P
    hb = (0 if fwd else SSD_H) + g * SSD_J
    edge = CHUNK - 1 if fwd else 0

    xs = xbc[:, g * 512:(g + 1) * 512].astype(F32)
    bm = xbc[:, SSD_W + g * D_STATE:SSD_W + (g + 1) * D_STATE]
    cm = xbc[:, SSD_W + GN + g * D_STATE:SSD_W + GN + (g + 1) * D_STATE]
    bt = bm.astype(F32).T.astype(BF16)
    if want_y:
        cbm = lax.dot_general(cm, bm, (((1,), (1,)), ((), ())), preferred_element_type=F32)
        yoff = jnp.dot(cm, state.astype(BF16), preferred_element_type=F32)

    def bcl(a, j):
        return jnp.broadcast_to(a[:, hb + j:hb + j + 1], (CHUNK, CHUNK))

    ys, xws, decs = [], [], []
    for kk in range(SSD_J // 2):
        j0, j1 = 2 * kk, 2 * kk + 1
        cc0, cc1 = bcl(cc, j0), bcl(cc, j1)
        cc_pair = jnp.where(lo, cc0, cc1)
        dt_pair = jnp.where(lo, bcl(dtc, j0), bcl(dtc, j1))
        xdt = xs[:, kk * 128:(kk + 1) * 128] * dt_pair
        tot = cc_pair[edge:edge + 1, :]
        xws.append((xdt * jnp.exp(tot - cc_pair)).astype(BF16))
        decs.append(jnp.exp(tot))
        if want_y:
            xdt_b = xdt.astype(BF16)

            def ydiag(ccj, j):
                crj = jnp.broadcast_to(cr[hb + j:hb + j + 1, :], (CHUNK, CHUNK))
                decay = jnp.exp(jnp.where(keep, ccj - crj, NEG_INF))
                return jnp.dot((cbm * decay).astype(BF16), xdt_b, preferred_element_type=F32)

            yd = jnp.where(lo, ydiag(cc0, j0), ydiag(cc1, j1))
            ys.append(yd + yoff[:, kk * 128:(kk + 1) * 128] * jnp.exp(cc_pair))
    xw = jnp.concatenate(xws, axis=1)
    dec = jnp.concatenate(decs, axis=1)
    new_state = state * dec + jnp.dot(bt, xw, preferred_element_type=F32)
    return new_state, (jnp.concatenate(ys, axis=1) if want_y else None)


def _ssd_bwd_kernel(xbc_ref, dt_ref, cc_ref, cr_ref, yb_ref, st_ref, *, ncs):
    s = pl.program_id(1)

    @pl.when(s == 0)
    def _():
        st_ref[...] = jnp.zeros_like(st_ref)

    for g in range(SSD_G):
        def body(i, carry, g=g):
            c = ncs - 1 - i
            r0 = pl.multiple_of(c * CHUNK, CHUNK)
            rs = pl.ds(r0, CHUNK)
            st, y = _ssd_chunk(st_ref[g], xbc_ref[rs, :], dt_ref[rs, :], cc_ref[rs, :], cr_ref[c],
                               g, False, True)
            st_ref[g] = st
            yb_ref[rs, g * 512:(g + 1) * 512] = y.astype(BF16)
            return carry

        lax.fori_loop(0, ncs, body, 0)


def _ssd_fwd_kernel(xbc_ref, dt_ref, cc_ref, cr_ref, z_ref, yb_ref, xm_ref, dtm_ref, ccm_ref, crm_ref,
                    dsk_ref, nw_ref, o_ref, st_ref, *, ncs):
    s = pl.program_id(1)

    @pl.when(s == 0)
    def _():
        for g in range(SSD_G):
            st, _ = _ssd_chunk(jnp.zeros((D_STATE, 512), F32), xm_ref[...], dtm_ref[...], ccm_ref[...],
                               crm_ref[0], g, True, False)
            st_ref[g] = st

    for g in range(SSD_G):
        gs = slice(g * 512, (g + 1) * 512)

        def body(c, carry, g=g, gs=gs):
            r0 = pl.multiple_of(c * CHUNK, CHUNK)
            rs = pl.ds(r0, CHUNK)
            xbc = xbc_ref[rs, :]
            st, y = _ssd_chunk(st_ref[g], xbc, dt_ref[rs, :], cc_ref[rs, :], cr_ref[c], g, True, True)
            st_ref[g] = st
            y = y + yb_ref[rs, gs].astype(F32) + dsk_ref[:, gs] * xbc[:, gs].astype(F32)
            z = z_ref[rs, gs].astype(F32)
            y = y * (z * _sigmoid(z))
            o_ref[rs, gs] = _rms(y, nw_ref[:, gs]).astype(BF16)
            return carry

        lax.fori_loop(0, ncs, body, 0)


def _ssd(xbc, dt, cc, cr, proj3, xbc_m, dt_m, cc_m, cr_m, dsk, nw, ts):
    bsz, seq, _ = xbc.shape
    ncs = ts // CHUNK
    nseg = seq // ts
    dt3 = dt.reshape(bsz, seq, 128)
    cc3 = cc.reshape(bsz, seq, 128)
    cr4 = cr.reshape(bsz, seq // CHUNK, 128, CHUNK)
    st = pltpu.VMEM((SSD_G, D_STATE, 512), F32)

    def seg_specs(rev):
        ix = (lambda s: nseg - 1 - s) if rev else (lambda s: s)
        return [pl.BlockSpec((None, ts, CONV_DIM), lambda b, s: (b, ix(s), 0)),
                pl.BlockSpec((None, ts, 128), lambda b, s: (b, ix(s), 0)),
                pl.BlockSpec((None, ts, 128), lambda b, s: (b, ix(s), 0)),
                pl.BlockSpec((None, ncs, 128, CHUNK), lambda b, s: (b, ix(s), 0, 0))]

    yb = pl.pallas_call(
        functools.partial(_ssd_bwd_kernel, ncs=ncs),
        out_shape=jax.ShapeDtypeStruct((bsz, seq, SSD_W), BF16),
        grid=(bsz, nseg),
        in_specs=seg_specs(True),
        out_specs=pl.BlockSpec((None, ts, SSD_W), lambda b, s: (b, nseg - 1 - s, 0)),
        scratch_shapes=[st],
        compiler_params=_cp(("parallel", "arbitrary")),
        name="ssd_bwd",
    )(xbc, dt3, cc3, cr4)

    return pl.pallas_call(
        functools.partial(_ssd_fwd_kernel, ncs=ncs),
        out_shape=jax.ShapeDtypeStruct((bsz, seq, SSD_W), BF16),
        grid=(bsz, nseg),
        in_specs=seg_specs(False) + [
            pl.BlockSpec((None, ts, SSD_W), lambda b, s: (b, s, C_Z // SSD_W)),
            pl.BlockSpec((None, ts, SSD_W), lambda b, s: (b, s, 0)),
            pl.BlockSpec((None, CHUNK, CONV_DIM), lambda b, s: (b, 0, 0)),
            pl.BlockSpec((CHUNK, 128), lambda b, s: (0, 0)),
            pl.BlockSpec((CHUNK, 128), lambda b, s: (0, 0)),
            pl.BlockSpec((1, 128, CHUNK), lambda b, s: (0, 0, 0)),
            pl.BlockSpec((1, SSD_W), lambda b, s: (0, 0)),
            pl.BlockSpec((1, SSD_W), lambda b, s: (0, 0))],
        out_specs=pl.BlockSpec((None, ts, SSD_W), lambda b, s: (b, s, 0)),
        scratch_shapes=[st],
        compiler_params=_cp(("parallel", "arbitrary")),
        name="ssd_fwd",
    )(xbc, dt3, cc3, cr4, proj3, yb, xbc_m, dt_m, cc_m, cr_m, dsk, nw)


def _out_kernel(o_ref, g_ref, ys_ref, mg_ref, x_ref, wa_ref, ws_ref, wo_ref, fnw_ref, out_ref):
    g = g_ref[...].astype(F32)
    ag = (o_ref[...].astype(F32) * (g * _sigmoid(g))).astype(BF16)
    ya = jnp.dot(ag, wa_ref[...], preferred_element_type=F32)
    ys = jnp.dot(ys_ref[...], ws_ref[...], preferred_element_type=F32)
    gate = _sigmoid(mg_ref[...].astype(F32))
    mixed = (gate[:, :D_MODEL] * ya + gate[:, D_MODEL:] * ys).astype(BF16)
    h = x_ref[...] + jnp.dot(mixed, wo_ref[...], preferred_element_type=F32)
    out_ref[...] = _rms(h, fnw_ref[...])


def _outproj(o2, proj, ys2, x2, wa, ws, wo, fnw, tm):
    rows = x2.shape[0]
    full = lambda shape: pl.BlockSpec(shape, lambda i: (0, 0))
    return pl.pallas_call(
        _out_kernel,
        out_shape=jax.ShapeDtypeStruct((rows, D_MODEL), F32),
        grid=(rows // tm,),
        in_specs=[pl.BlockSpec((tm, ATTN_W), lambda i: (i, 0)),
                  pl.BlockSpec((tm, ATTN_W), lambda i: (i, C_G // ATTN_W)),
                  pl.BlockSpec((tm, SSD_W), lambda i: (i, 0)),
                  pl.BlockSpec((tm, 2 * D_MODEL), lambda i: (i, C_MG // (2 * D_MODEL))),
                  pl.BlockSpec((tm, D_MODEL), lambda i: (i, 0)),
                  full((ATTN_W, D_MODEL)), full((SSD_W, D_MODEL)), full((D_MODEL, D_MODEL)),
                  full((1, D_MODEL))],
        out_specs=pl.BlockSpec((tm, D_MODEL), lambda i: (i, 0)),
        compiler_params=_cp(("parallel",)),
        name="outproj",
    )(o2, proj, ys2, proj, x2, wa, ws, wo, fnw)


def _rope_tables(pos):
    inv_freq = ROPE_THETA ** (-jnp.arange(0, ROPE, 2, dtype=F32) / ROPE)
    ang = pos.astype(F32)[:, None] * inv_freq[None, :]
    cos, sin = jnp.cos(ang), jnp.sin(ang)
    n = pos.shape[0]
    one = jnp.ones((n, NOPE), F32)
    z16 = jnp.zeros((n, ROPE // 2), F32)
    z32 = jnp.zeros((n, HP - QK_DIM), F32)
    z64 = jnp.zeros((n, NOPE), F32)
    cos_t = jnp.concatenate([one, cos, cos, z32 + 1.0], axis=1)
    sin_p = jnp.concatenate([z64, z16, sin, z32], axis=1)
    sin_m = jnp.concatenate([z64, -sin, z16, z32], axis=1)
    return cos_t, sin_p, sin_m


def _pick(n, prefs):
    for t in prefs:
        if n % t == 0:
            return t
    raise ValueError(f"no tile for {n}")


def kernel(x, meta_tokens, norm_w, w_in, q_norm_w, w_uq, kv_norm_w, w_ukv, w_attn_proj, conv_w, conv_b,
           dt_bias, a_log, d_skip, ssd_norm_w, w_ssd_proj, w_out, final_norm_w):
    bsz, seq, d = x.shape
    assert d == D_MODEL and seq % 128 == 0 and norm_w.shape[0] == 1
    rows = bsz * seq
    zc = lambda n: jnp.zeros((D_MODEL, n), F32)
    w = w_in[0]
    w_p = jnp.concatenate([
        w[:, O_KV:O_KV + KV_RANK], zc(NOPE), w[:, O_KR:O_KR + ROPE], zc(HP - QK_DIM),
        w[:, O_Q:O_Q + Q_RANK], w[:, O_DT:O_DT + 2 * SSD_H], zc(128 - 2 * SSD_H), zc(128),
        w[:, O_G:O_G + ATTN_W], w[:, O_Z:O_Z + SSD_W], w[:, O_XBC:O_XBC + SSD_W],
        w[:, O_MG:O_MG + 2 * D_MODEL], w[:, O_XBC + SSD_W:O_XBC + CONV_DIM]], axis=1).astype(BF16)
    assert w_p.shape[1] == NP
    pad_h = lambda t: jnp.pad(t, ((0, 0), (0, 0), (0, HP - t.shape[-1]))).reshape(t.shape[0], N_HEADS * HP)
    wq = pad_h(w_uq[0].reshape(Q_RANK, N_HEADS, QK_DIM)).astype(BF16)
    ukv = w_ukv[0].reshape(KV_RANK, N_HEADS, NOPE + V_DIM)
    wkv = jnp.concatenate([pad_h(ukv[..., :NOPE]), pad_h(ukv[..., NOPE:])], axis=1).astype(BF16)
    row = lambda t: t.reshape(1, -1)
    pad128 = lambda t: jnp.pad(t.reshape(1, -1), ((0, 0), (0, 128 - t.size)))

    tm = _pick(rows, (1024, 512, 256, 128))
    x2 = x.reshape(rows, D_MODEL)
    proj, dt_raw = _inproj(x2, row(norm_w[0]), w_p, tm)
    xm = jnp.concatenate([jnp.zeros((META_PAD, D_MODEL), F32), meta_tokens.astype(F32)], axis=0)
    proj_m, dt_raw_m = _inproj(xm, row(norm_w[0]), w_p, CHUNK)

    tabs = _rope_tables(N_META + jnp.arange(seq))
    tabs_m = _rope_tables(jnp.maximum(jnp.arange(CHUNK) - META_PAD, 0))
    tq2 = _pick(seq, (512, 256, 128))
    q, k, v = _qkv(proj, tabs, row(q_norm_w[0]), row(kv_norm_w[0]), wq, wkv, tq2, seq // tq2)
    _, km, vm = _qkv(proj_m, tabs_m, row(q_norm_w[0]), row(kv_norm_w[0]), wq, wkv, CHUNK, 1)

    r3 = lambda t: t.reshape(bsz, seq, t.shape[-1])
    tq = _pick(seq, (1024, 512, 256, 128))
    tk = _pick(seq, (512, 256, 128))
    o = _flash(r3(q), r3(k), r3(v), km, vm, tq, tk)

    proj3 = r3(proj)
    cw, cb = conv_w[0], row(conv_b[0])
    xbc = _conv(proj3, proj_m, cw, cb, _pick(seq, (1024, 512, 256, 128)))
    xbc_m = _conv_meta(proj3, proj_m, cw, cb)
    bias, alog = pad128(dt_bias[0]), pad128(a_log[0])
    dt, cc, cr = _dtprep(dt_raw, bias, alog, _pick(rows // CHUNK, (8, 4, 2, 1)), 0)
    dt_m, cc_m, cr_m = _dtprep(dt_raw_m, bias, alog, 1, META_PAD)
    dsk = row(jnp.repeat(d_skip[0], SSD_P))
    ys = _ssd(xbc, dt, cc, cr, proj3, xbc_m, dt_m, cc_m, cr_m, dsk, row(ssd_norm_w[0]),
              _pick(seq, (512, 256, 128)))

    out = _outproj(o.reshape(rows, ATTN_W), proj, ys.reshape(rows, SSD_W), x2,
                   w_attn_proj[0].astype(BF16), w_ssd_proj[0].astype(BF16), w_out[0].astype(BF16),
                   row(final_norm_w), _pick(rows, (512, 256, 128)))
    return out.reshape(bsz, seq, D_MODEL)
```

```python
import functools
import math

import jax
import jax.numpy as jnp
from jax import lax
from jax.experimental import pallas as pl
from jax.experimental.pallas import tpu as pltpu

F32 = jnp.float32
BF16 = jnp.bfloat16

D_MODEL = 1024
N_META = 16
N_HEADS = 16
NOPE = 64
ROPE = 32
QK_DIM = NOPE + ROPE
V_DIM = 64
Q_RANK = 384
KV_RANK = 256
ROPE_THETA = 10000.0
ATTN_W = N_HEADS * V_DIM
SSD_W = 2048
SSD_P = 64
SSD_H = SSD_W // SSD_P
SSD_G = 4
SSD_J = SSD_H // SSD_G
D_STATE = 128
CONV_K = 5
GN = SSD_G * D_STATE
CONV_DIM = SSD_W + 2 * GN
CHUNK = 128
EPS = 1e-6
HP = 128
META_PAD = CHUNK - N_META

C_KV, C_KR, C_Q, C_DT = 0, 256, 384, 768
C_G, C_Z, C_XS, C_MG, C_B, C_C = 1024, 2048, 4096, 6144, 8192, 8704
NP = 9216
O_Q, O_KV, O_KR, O_G, O_Z, O_XBC, O_DT, O_MG = 0, 384, 640, 672, 1696, 3744, 6816, 6880

VMEM_LIMIT = 56 * 1024 * 1024
NEG_INF = float("-inf")
LOG2E = math.log2(math.e)


def _cp(sem):
    return pltpu.CompilerParams(dimension_semantics=sem, vmem_limit_bytes=VMEM_LIMIT)


def _sigmoid(x):
    return 1.0 / (1.0 + jnp.exp(-x))


def _rms(x, w):
    return x * lax.rsqrt(jnp.mean(x * x, axis=-1, keepdims=True) + EPS) * w


def _inproj_kernel(x_ref, nw_ref, w_ref, o_ref, dt_ref, u_ref):
    j = pl.program_id(1)

    @pl.when(j == 0)
    def _():
        u_ref[...] = _rms(x_ref[...], nw_ref[...]).astype(BF16)

    acc = jnp.dot(u_ref[...], w_ref[...], preferred_element_type=F32)
    o_ref[...] = acc.astype(BF16)

    @pl.when(j == 0)
    def _():
        dt_ref[...] = acc[:, C_DT:C_DT + 128]


def _inproj(x2, nw, w, tm, tn=1024):
    rows = x2.shape[0]
    return pl.pallas_call(
        _inproj_kernel,
        out_shape=(jax.ShapeDtypeStruct((rows, NP), BF16),
                   jax.ShapeDtypeStruct((rows, 128), F32)),
        grid=(rows // tm, NP // tn),
        in_specs=[pl.BlockSpec((tm, D_MODEL), lambda i, j: (i, 0)),
                  pl.BlockSpec((1, D_MODEL), lambda i, j: (0, 0)),
                  pl.BlockSpec((D_MODEL, tn), lambda i, j: (0, j))],
        out_specs=(pl.BlockSpec((tm, tn), lambda i, j: (i, j)),
                   pl.BlockSpec((tm, 128), lambda i, j: (i, 0))),
        scratch_shapes=[pltpu.VMEM((tm, D_MODEL), BF16)],
        compiler_params=_cp(("parallel", "arbitrary")),
        name="inproj",
    )(x2, nw, w)


def _qkv_kernel(hd_ref, cos_ref, sp_ref, sm_ref, qnw_ref, kvnw_ref, wq_ref, wkv_ref,
                q_ref, k_ref, v_ref, *, scale):
    cos_t, sin_p, sin_m = cos_ref[...], sp_ref[...], sm_ref[...]

    def rope(t):
        return t * cos_t + pltpu.roll(t, 16, 1) * sin_p + pltpu.roll(t, 112, 1) * sin_m

    kv_lat = hd_ref[:, C_KV:C_KV + KV_RANK].astype(F32)
    kvn = _rms(kv_lat, kvnw_ref[...]).astype(BF16)
    kv = jnp.dot(kvn, wkv_ref[...], preferred_element_type=F32)
    kpe = rope(hd_ref[:, C_KR:C_KR + HP].astype(F32))
    for h in range(N_HEADS):
        sl = slice(h * HP, (h + 1) * HP)
        k_ref[:, sl] = (kv[:, sl] + kpe).astype(BF16)
    vv = kv[:, N_HEADS * HP:]
    lane = lax.broadcasted_iota(jnp.int32, vv.shape, 1)
    v_ref[...] = jnp.where(lane % HP == V_DIM, 1.0, vv).astype(BF16)

    q_lat = hd_ref[:, C_Q:C_Q + Q_RANK].astype(F32)
    qn = _rms(q_lat, qnw_ref[...]).astype(BF16)
    q = jnp.dot(qn, wq_ref[...], preferred_element_type=F32)
    for h in range(N_HEADS):
        sl = slice(h * HP, (h + 1) * HP)
        q_ref[:, sl] = (rope(q[:, sl]) * scale).astype(BF16)


def _qkv(proj, tabs, qnw, kvnw, wq, wkv, tm, tiles_per_seq):
    rows = proj.shape[0]
    w = N_HEADS * HP
    tab_spec = pl.BlockSpec((tm, HP), lambda i: (i % tiles_per_seq, 0))
    out = jax.ShapeDtypeStruct((rows, w), BF16)
    return pl.pallas_call(
        functools.partial(_qkv_kernel, scale=LOG2E * QK_DIM ** -0.5),
        out_shape=(out, out, out),
        grid=(rows // tm,),
        in_specs=[pl.BlockSpec((tm, 1024), lambda i: (i, 0)),
                  tab_spec, tab_spec, tab_spec,
                  pl.BlockSpec((1, Q_RANK), lambda i: (0, 0)),
                  pl.BlockSpec((1, KV_RANK), lambda i: (0, 0)),
                  pl.BlockSpec((Q_RANK, w), lambda i: (0, 0)),
                  pl.BlockSpec((KV_RANK, 2 * w), lambda i: (0, 0))],
        out_specs=(pl.BlockSpec((tm, w), lambda i: (i, 0)),) * 3,
        compiler_params=_cp(("parallel",)),
        name="qkv",
    )(proj, *tabs, qnw, kvnw, wq, wkv)


def _flash_kernel(q_ref, k_ref, v_ref, km_ref, vm_ref, o_ref, m_ref, acc_ref, *, tk, nk):
    nt = (((1,), (1,)), ((), ()))
    heads = [slice(hh * HP, (hh + 1) * HP) for hh in range(2)]
    for hh, sl in enumerate(heads):
        s0 = lax.dot_general(q_ref[:, sl], km_ref[:, sl], nt, preferred_element_type=F32)
        col = lax.broadcasted_iota(jnp.int32, s0.shape, 1)
        s0 = jnp.where(col >= META_PAD, s0, NEG_INF)
        m = jnp.broadcast_to(jnp.max(s0, axis=-1, keepdims=True), s0.shape)
        m_ref[hh] = m
        acc_ref[hh] = jnp.dot(jnp.exp2(s0 - m).astype(BF16), vm_ref[:, sl], preferred_element_type=F32)

    def body(j, carry):
        off = pl.multiple_of(j * tk, tk)
        ss = [lax.dot_general(q_ref[:, sl], k_ref[pl.ds(off, tk), sl], nt, preferred_element_type=F32)
              for sl in heads]
        for hh, sl in enumerate(heads):
            blocks = [ss[hh][:, c * HP:(c + 1) * HP] for c in range(tk // HP)]
            m = m_ref[hh]
            m_new = jnp.maximum(m, jnp.max(functools.reduce(jnp.maximum, blocks), axis=-1, keepdims=True))
            p = jnp.concatenate([jnp.exp2(blk - m_new) for blk in blocks], axis=-1).astype(BF16)
            pv = jnp.dot(p, v_ref[pl.ds(off, tk), sl], preferred_element_type=F32)
            acc_ref[hh] = jnp.exp2(m - m_new) * acc_ref[hh] + pv
            m_ref[hh] = m_new
        return carry

    lax.fori_loop(0, nk, body, 0, unroll=2)
    outs = [acc_ref[hh][:, :V_DIM] / acc_ref[hh][:, V_DIM:V_DIM + 1] for hh in range(2)]
    o_ref[...] = jnp.concatenate(outs, axis=-1).astype(BF16)


def _flash(q, k, v, km, vm, tq, tk):
    bsz, seq, _ = q.shape
    return pl.pallas_call(
        functools.partial(_flash_kernel, tk=tk, nk=seq // tk),
        out_shape=jax.ShapeDtypeStruct((bsz, seq, ATTN_W), BF16),
        grid=(bsz, N_HEADS // 2, seq // tq),
        in_specs=[pl.BlockSpec((None, tq, 2 * HP), lambda b, h, i: (b, i, h)),
                  pl.BlockSpec((None, seq, 2 * HP), lambda b, h, i: (b, 0, h)),
                  pl.BlockSpec((None, seq, 2 * HP), lambda b, h, i: (b, 0, h)),
                  pl.BlockSpec((CHUNK, 2 * HP), lambda b, h, i: (0, h)),
                  pl.BlockSpec((CHUNK, 2 * HP), lambda b, h, i: (0, h))],
        out_specs=pl.BlockSpec((None, tq, 2 * V_DIM), lambda b, h, i: (b, i, h)),
        scratch_shapes=[pltpu.VMEM((2, tq, HP), F32), pltpu.VMEM((2, tq, HP), F32)],
        compiler_params=_cp(("parallel", "parallel", "arbitrary")),
        name="flash",
    )(q, k, v, km, vm)


def _conv_taps(ext_ref, w_ref, b_ref, rows):
    ext = ext_ref[...]
    n = ext.shape[0]
    acc = b_ref[...] + w_ref[CONV_K // 2:CONV_K // 2 + 1, :] * ext[16:16 + rows]
    for t in range(CONV_K):
        if t != CONV_K // 2:
            shifted = pltpu.roll(ext, (CONV_K // 2 - t) % n, 0)
            acc = acc + w_ref[t:t + 1, :] * shifted[16:16 + rows]
    return acc * _sigmoid(acc)


def _conv_kernel(main_ref, prev_ref, next_ref, meta_ref, w_ref, b_ref, o_ref, ext_ref, *, tr, nr):
    r = pl.program_id(2)
    prev = jnp.where(r == 0, meta_ref[CHUNK - 16:CHUNK, :], prev_ref[...])
    nxt = jnp.where(r == nr - 1, jnp.zeros_like(next_ref[...]), next_ref[...])
    ext_ref[0:16, :] = prev.astype(F32)
    ext_ref[16:16 + tr, :] = main_ref[...].astype(F32)
    ext_ref[16 + tr:32 + tr, :] = nxt.astype(F32)
    o_ref[...] = _conv_taps(ext_ref, w_ref, b_ref, tr).astype(BF16)


def _conv_col(c):
    return jnp.where(c < 4, C_XS // 512 + c, C_B // 512 + (c - 4))


def _conv(proj3, proj_meta, cw, cb, tr):
    bsz, seq, _ = proj3.shape
    nr = seq // tr
    hb = tr // 16
    return pl.pallas_call(
        functools.partial(_conv_kernel, tr=tr, nr=nr),
        out_shape=jax.ShapeDtypeStruct((bsz, seq, CONV_DIM), BF16),
        grid=(bsz, CONV_DIM // 512, nr),
        in_specs=[pl.BlockSpec((None, tr, 512), lambda b, c, r: (b, r, _conv_col(c))),
                  pl.BlockSpec((None, 16, 512),
                               lambda b, c, r: (b, jnp.maximum(r * hb - 1, 0), _conv_col(c))),
                  pl.BlockSpec((None, 16, 512),
                               lambda b, c, r: (b, jnp.minimum((r + 1) * hb, seq // 16 - 1), _conv_col(c))),
                  pl.BlockSpec((CHUNK, 512), lambda b, c, r: (0, _conv_col(c))),
                  pl.BlockSpec((CONV_K, 512), lambda b, c, r: (0, c)),
                  pl.BlockSpec((1, 512), lambda b, c, r: (0, c))],
        out_specs=pl.BlockSpec((None, tr, 512), lambda b, c, r: (b, r, c)),
        scratch_shapes=[pltpu.VMEM((tr + 32, 512), F32)],
        compiler_params=_cp(("parallel", "parallel", "arbitrary")),
        name="conv",
    )(proj3, proj3, proj3, proj_meta, cw, cb)


def _conv_meta_kernel(meta_ref, main_ref, w_ref, b_ref, o_ref, ext_ref):
    ext_ref[0:32, :] = meta_ref[CHUNK - 32:CHUNK, :].astype(F32)
    ext_ref[32:48, :] = main_ref[...].astype(F32)
    o_ref[0:META_PAD, :] = jnp.zeros((META_PAD, 512), BF16)
    o_ref[META_PAD:CHUNK, :] = _conv_taps(ext_ref, w_ref, b_ref, N_META).astype(BF16)


def _conv_meta(proj3, proj_meta, cw, cb):
    bsz = proj3.shape[0]
    return pl.pallas_call(
        _conv_meta_kernel,
        out_shape=jax.ShapeDtypeStruct((bsz, CHUNK, CONV_DIM), BF16),
        grid=(bsz, CONV_DIM // 512),
        in_specs=[pl.BlockSpec((CHUNK, 512), lambda b, c: (0, _conv_col(c))),
                  pl.BlockSpec((None, 16, 512), lambda b, c: (b, 0, _conv_col(c))),
                  pl.BlockSpec((CONV_K, 512), lambda b, c: (0, c)),
                  pl.BlockSpec((1, 512), lambda b, c: (0, c))],
        out_specs=pl.BlockSpec((None, CHUNK, 512), lambda b, c: (b, 0, c)),
        scratch_shapes=[pltpu.VMEM((48, 512), F32)],
        compiler_params=_cp(("parallel", "parallel")),
        name="conv_meta",
    )(proj_meta, proj3, cw, cb)


def _dt_kernel(raw_ref, bias_ref, alog_ref, dt_ref, cc_ref, cr_ref, *, cpb, valid_from):
    row = lax.broadcasted_iota(jnp.int32, (CHUNK, 128), 0)
    col = lax.broadcasted_iota(jnp.int32, (CHUNK, 128), 1)
    tri = (col <= row).astype(F32)
    triu = (col >= row).astype(F32)
    a = -jnp.exp(alog_ref[...]) * LOG2E
    for c in range(cpb):
        rs = slice(c * CHUNK, (c + 1) * CHUNK)
        x = raw_ref[rs, :] + bias_ref[...]
        dt = jnp.maximum(x, 0.0) + jnp.log(1.0 + jnp.exp(-jnp.abs(x)))
        dt = jnp.where(col < 2 * SSD_H, dt, 0.0)
        if valid_from:
            dt = jnp.where(row >= valid_from, dt, 0.0)
        da = dt * a
        cf = jnp.dot(tri, da, precision=lax.Precision.HIGHEST, preferred_element_type=F32)
        cb = jnp.dot(triu, da, precision=lax.Precision.HIGHEST, preferred_element_type=F32)
        cum = jnp.where(col < SSD_H, cf, cb)
        dt_ref[rs, :] = dt
        cc_ref[rs, :] = cum
        cr_ref[c] = cum.T


def _dtprep(raw, bias, alog, cpb, valid_from):
    rows = raw.shape[0]
    nch = rows // CHUNK
    blk = cpb * CHUNK
    o = jax.ShapeDtypeStruct((rows, 128), F32)
    return pl.pallas_call(
        functools.partial(_dt_kernel, cpb=cpb, valid_from=valid_from),
        out_shape=(o, o, jax.ShapeDtypeStruct((nch, 128, CHUNK), F32)),
        grid=(nch // cpb,),
        in_specs=[pl.BlockSpec((blk, 128), lambda i: (i, 0)),
                  pl.BlockSpec((1, 128), lambda i: (0, 0)),
                  pl.BlockSpec((1, 128), lambda i: (0, 0))],
        out_specs=(pl.BlockSpec((blk, 128), lambda i: (i, 0)),
                   pl.BlockSpec((blk, 128), lambda i: (i, 0)),
                   pl.BlockSpec((cpb, 128, CHUNK), lambda i: (i, 0, 0))),
        compiler_params=_cp(("parallel",)),
        name="dtprep",
    )(raw, bias, alog)


def _ssd_chunk(state, xbc, dtc, cc, cr, g, fwd, want_y):
    row = lax.broadcasted_iota(jnp.int32, (CHUNK, CHUNK), 0)
    col = lax.broadcasted_iota(jnp.int32, (CHUNK, CHUNK), 1)
    keep = (col <= row) if fwd else (col >= row)
    lo = col < SSD_P
    hb = (0 if fwd else SSD_H) + g * SSD_J
    edge = CHUNK - 1 if fwd else 0

    xs = xbc[:, g * 512:(g + 1) * 512].astype(F32)
    bm = xbc[:, SSD_W + g * D_STATE:SSD_W + (g + 1) * D_STATE]
    cm = xbc[:, SSD_W + GN + g * D_STATE:SSD_W + GN + (g + 1) * D_STATE]
    bt = bm.astype(F32).T.astype(BF16)
    if want_y:
        cbm = lax.dot_general(cm, bm, (((1,), (1,)), ((), ())), preferred_element_type=F32)
        yoff = jnp.dot(cm, state.astype(BF16), preferred_element_type=F32)

    def bcl(a, j):
        return jnp.broadcast_to(a[:, hb + j:hb + j + 1], (CHUNK, CHUNK))

    ys, xws, decs = [], [], []
    for kk in range(SSD_J // 2):
        j0, j1 = 2 * kk, 2 * kk + 1
        cc0, cc1 = bcl(cc, j0), bcl(cc, j1)
        cc_pair = jnp.where(lo, cc0, cc1)
        dt_pair = jnp.take_along_axis(dtc, jnp.where(lo, hb + j0, hb + j1), axis=1)
        xdt = xs[:, kk * 128:(kk + 1) * 128] * dt_pair
        tot = cc_pair[edge:edge + 1, :]
        xws.append((xdt * jnp.exp2(tot - cc_pair)).astype(BF16))
        decs.append(jnp.exp2(tot))
        if want_y:
            xdt_b = xdt.astype(BF16)

            def ydiag(ccj, j):
                crj = jnp.broadcast_to(cr[hb + j:hb + j + 1, :], (CHUNK, CHUNK))
                decay = jnp.exp2(jnp.where(keep, ccj - crj, NEG_INF))
                return jnp.dot((cbm * decay).astype(BF16), xdt_b, preferred_element_type=F32)

            yd = jnp.where(lo, ydiag(cc0, j0), ydiag(cc1, j1))
            ys.append(yd + yoff[:, kk * 128:(kk + 1) * 128] * jnp.exp2(cc_pair))
    xw = jnp.concatenate(xws, axis=1)
    dec = jnp.concatenate(decs, axis=1)
    new_state = state * dec + jnp.dot(bt, xw, preferred_element_type=F32)
    return new_state, (jnp.concatenate(ys, axis=1) if want_y else None)


def _ssd_bwd_kernel(xbc_ref, dt_ref, cc_ref, cr_ref, yb_ref, st_ref, *, ncs):
    s = pl.program_id(1)

    @pl.when(s == 0)
    def _():
        st_ref[...] = jnp.zeros_like(st_ref)

    def body(i, carry):
        c = ncs - 1 - i
        r0 = pl.multiple_of(c * CHUNK, CHUNK)
        rs = pl.ds(r0, CHUNK)
        for g in range(SSD_G):
            st, y = _ssd_chunk(st_ref[g], xbc_ref[rs, :], dt_ref[rs, :], cc_ref[rs, :], cr_ref[c],
                               g, False, True)
            st_ref[g] = st
            yb_ref[rs, g * 512:(g + 1) * 512] = y.astype(BF16)
        return carry

    lax.fori_loop(0, ncs, body, 0)


def _ssd_fwd_kernel(xbc_ref, dt_ref, cc_ref, cr_ref, z_ref, yb_ref, xm_ref, dtm_ref, ccm_ref, crm_ref,
                    dsk_ref, nw_ref, o_ref, st_ref, *, ncs):
    s = pl.program_id(1)

    @pl.when(s == 0)
    def _():
        for g in range(SSD_G):
            st, _ = _ssd_chunk(jnp.zeros((D_STATE, 512), F32), xm_ref[...], dtm_ref[...], ccm_ref[...],
                               crm_ref[0], g, True, False)
            st_ref[g] = st

    def body(c, carry):
        r0 = pl.multiple_of(c * CHUNK, CHUNK)
        rs = pl.ds(r0, CHUNK)
        for g in range(SSD_G):
            gs = slice(g * 512, (g + 1) * 512)
            xbc = xbc_ref[rs, :]
            st, y = _ssd_chunk(st_ref[g], xbc, dt_ref[rs, :], cc_ref[rs, :], cr_ref[c], g, True, True)
            st_ref[g] = st
            y = y + yb_ref[rs, gs].astype(F32) + dsk_ref[:, gs] * xbc[:, gs].astype(F32)
            z = z_ref[rs, gs].astype(F32)
            y = y * (z * _sigmoid(z))
            o_ref[rs, gs] = _rms(y, nw_ref[:, gs]).astype(BF16)
        return carry

    lax.fori_loop(0, ncs, body, 0)


def _ssd(xbc, dt, cc, cr, proj3, xbc_m, dt_m, cc_m, cr_m, dsk, nw, ts):
    bsz, seq, _ = xbc.shape
    ncs = ts // CHUNK
    nseg = seq // ts
    dt3 = dt.reshape(bsz, seq, 128)
    cc3 = cc.reshape(bsz, seq, 128)
    cr4 = cr.reshape(bsz, seq // CHUNK, 128, CHUNK)
    st = pltpu.VMEM((SSD_G, D_STATE, 512), F32)

    def seg_specs(rev):
        ix = (lambda s: nseg - 1 - s) if rev else (lambda s: s)
        return [pl.BlockSpec((None, ts, CONV_DIM), lambda b, s: (b, ix(s), 0)),
                pl.BlockSpec((None, ts, 128), lambda b, s: (b, ix(s), 0)),
                pl.BlockSpec((None, ts, 128), lambda b, s: (b, ix(s), 0)),
                pl.BlockSpec((None, ncs, 128, CHUNK), lambda b, s: (b, ix(s), 0, 0))]

    yb = pl.pallas_call(
        functools.partial(_ssd_bwd_kernel, ncs=ncs),
        out_shape=jax.ShapeDtypeStruct((bsz, seq, SSD_W), BF16),
        grid=(bsz, nseg),
        in_specs=seg_specs(True),
        out_specs=pl.BlockSpec((None, ts, SSD_W), lambda b, s: (b, nseg - 1 - s, 0)),
        scratch_shapes=[st],
        compiler_params=_cp(("parallel", "arbitrary")),
        name="ssd_bwd",
    )(xbc, dt3, cc3, cr4)

    return pl.pallas_call(
        functools.partial(_ssd_fwd_kernel, ncs=ncs),
        out_shape=jax.ShapeDtypeStruct((bsz, seq, SSD_W), BF16),
        grid=(bsz, nseg),
        in_specs=seg_specs(False) + [
            pl.BlockSpec((None, ts, SSD_W), lambda b, s: (b, s, C_Z // SSD_W)),
            pl.BlockSpec((None, ts, SSD_W), lambda b, s: (b, s, 0)),
            pl.BlockSpec((None, CHUNK, CONV_DIM), lambda b, s: (b, 0, 0)),
            pl.BlockSpec((CHUNK, 128), lambda b, s: (0, 0)),
            pl.BlockSpec((CHUNK, 128), lambda b, s: (0, 0)),
            pl.BlockSpec((1, 128, CHUNK), lambda b, s: (0, 0, 0)),
            pl.BlockSpec((1, SSD_W), lambda b, s: (0, 0)),
            pl.BlockSpec((1, SSD_W), lambda b, s: (0, 0))],
        out_specs=pl.BlockSpec((None, ts, SSD_W), lambda b, s: (b, s, 0)),
        scratch_shapes=[st],
        compiler_params=_cp(("parallel", "arbitrary")),
        name="ssd_fwd",
    )(xbc, dt3, cc3, cr4, proj3, yb, xbc_m, dt_m, cc_m, cr_m, dsk, nw)


def _out_kernel(o_ref, g_ref, ys_ref, mg_ref, x_ref, wa_ref, ws_ref, wo_ref, fnw_ref, out_ref):
    g = g_ref[...].astype(F32)
    ag = (o_ref[...].astype(F32) * (g * _sigmoid(g))).astype(BF16)
    ya = jnp.dot(ag, wa_ref[...], preferred_element_type=F32)
    ys = jnp.dot(ys_ref[...], ws_ref[...], preferred_element_type=F32)
    gate = _sigmoid(mg_ref[...].astype(F32))
    mixed = (gate[:, :D_MODEL] * ya + gate[:, D_MODEL:] * ys).astype(BF16)
    h = x_ref[...] + jnp.dot(mixed, wo_ref[...], preferred_element_type=F32)
    out_ref[...] = _rms(h, fnw_ref[...])


def _outproj(o2, proj, ys2, x2, wa, ws, wo, fnw, tm):
    rows = x2.shape[0]
    full = lambda shape: pl.BlockSpec(shape, lambda i: (0, 0))
    return pl.pallas_call(
        _out_kernel,
        out_shape=jax.ShapeDtypeStruct((rows, D_MODEL), F32),
        grid=(rows // tm,),
        in_specs=[pl.BlockSpec((tm, ATTN_W), lambda i: (i, 0)),
                  pl.BlockSpec((tm, ATTN_W), lambda i: (i, C_G // ATTN_W)),
                  pl.BlockSpec((tm, SSD_W), lambda i: (i, 0)),
                  pl.BlockSpec((tm, 2 * D_MODEL), lambda i: (i, C_MG // (2 * D_MODEL))),
                  pl.BlockSpec((tm, D_MODEL), lambda i: (i, 0)),
                  full((ATTN_W, D_MODEL)), full((SSD_W, D_MODEL)), full((D_MODEL, D_MODEL)),
                  full((1, D_MODEL))],
        out_specs=pl.BlockSpec((tm, D_MODEL), lambda i: (i, 0)),
        compiler_params=_cp(("parallel",)),
        name="outproj",
    )(o2, proj, ys2, proj, x2, wa, ws, wo, fnw)


def _rope_tables(pos):
    inv_freq = ROPE_THETA ** (-jnp.arange(0, ROPE, 2, dtype=F32) / ROPE)
    ang = pos.astype(F32)[:, None] * inv_freq[None, :]
    cos, sin = jnp.cos(ang), jnp.sin(ang)
    n = pos.shape[0]
    one = jnp.ones((n, NOPE), F32)
    z16 = jnp.zeros((n, ROPE // 2), F32)
    z32 = jnp.zeros((n, HP - QK_DIM), F32)
    z64 = jnp.zeros((n, NOPE), F32)
    cos_t = jnp.concatenate([one, cos, cos, z32 + 1.0], axis=1)
    sin_p = jnp.concatenate([z64, z16, sin, z32], axis=1)
    sin_m = jnp.concatenate([z64, -sin, z16, z32], axis=1)
    return cos_t, sin_p, sin_m


def _pick(n, prefs):
    for t in prefs:
        if n % t == 0:
            return t
    raise ValueError(f"no tile for {n}")


def kernel(x, meta_tokens, norm_w, w_in, q_norm_w, w_uq, kv_norm_w, w_ukv, w_attn_proj, conv_w, conv_b,
           dt_bias, a_log, d_skip, ssd_norm_w, w_ssd_proj, w_out, final_norm_w):
    bsz, seq, d = x.shape
    assert d == D_MODEL and seq % 128 == 0 and norm_w.shape[0] == 1
    rows = bsz * seq
    zc = lambda n: jnp.zeros((D_MODEL, n), F32)
    w = w_in[0]
    w_p = jnp.concatenate([
        w[:, O_KV:O_KV + KV_RANK], zc(NOPE), w[:, O_KR:O_KR + ROPE], zc(HP - QK_DIM),
        w[:, O_Q:O_Q + Q_RANK], w[:, O_DT:O_DT + 2 * SSD_H], zc(128 - 2 * SSD_H), zc(128),
        w[:, O_G:O_G + ATTN_W], w[:, O_Z:O_Z + SSD_W], w[:, O_XBC:O_XBC + SSD_W],
        w[:, O_MG:O_MG + 2 * D_MODEL], w[:, O_XBC + SSD_W:O_XBC + CONV_DIM]], axis=1).astype(BF16)
    assert w_p.shape[1] == NP
    pad_h = lambda t: jnp.pad(t, ((0, 0), (0, 0), (0, HP - t.shape[-1]))).reshape(t.shape[0], N_HEADS * HP)
    wq = pad_h(w_uq[0].reshape(Q_RANK, N_HEADS, QK_DIM)).astype(BF16)
    ukv = w_ukv[0].reshape(KV_RANK, N_HEADS, NOPE + V_DIM)
    wkv = jnp.concatenate([pad_h(ukv[..., :NOPE]), pad_h(ukv[..., NOPE:])], axis=1).astype(BF16)
    row = lambda t: t.reshape(1, -1)
    pad128 = lambda t: jnp.pad(t.reshape(1, -1), ((0, 0), (0, 128 - t.size)))

    tm = _pick(rows, (1024, 512, 256, 128))
    x2 = x.reshape(rows, D_MODEL)
    proj, dt_raw = _inproj(x2, row(norm_w[0]), w_p, tm)
    xm = jnp.concatenate([jnp.zeros((META_PAD, D_MODEL), F32), meta_tokens.astype(F32)], axis=0)
    proj_m, dt_raw_m = _inproj(xm, row(norm_w[0]), w_p, CHUNK)

    tabs = _rope_tables(N_META + jnp.arange(seq))
    tabs_m = _rope_tables(jnp.maximum(jnp.arange(CHUNK) - META_PAD, 0))
    tq2 = _pick(seq, (512, 256, 128))
    q, k, v = _qkv(proj, tabs, row(q_norm_w[0]), row(kv_norm_w[0]), wq, wkv, tq2, seq // tq2)
    _, km, vm = _qkv(proj_m, tabs_m, row(q_norm_w[0]), row(kv_norm_w[0]), wq, wkv, CHUNK, 1)

    r3 = lambda t: t.reshape(bsz, seq, t.shape[-1])
    tq = _pick(seq, (1024, 512, 256, 128))
    tk = _pick(seq, (512, 256, 128))
    o = _flash(r3(q), r3(k), r3(v), km, vm, tq, tk)

    proj3 = r3(proj)
    cw, cb = conv_w[0], row(conv_b[0])
    xbc = _conv(proj3, proj_m, cw, cb, _pick(seq, (1024, 512, 256, 128)))
    xbc_m = _conv_meta(proj3, proj_m, cw, cb)
    bias, alog = pad128(dt_bias[0]), pad128(a_log[0])
    dt, cc, cr = _dtprep(dt_raw, bias, alog, _pick(rows // CHUNK, (8, 4, 2, 1)), 0)
    dt_m, cc_m, cr_m = _dtprep(dt_raw_m, bias, alog, 1, META_PAD)
    dsk = row(jnp.repeat(d_skip[0], SSD_P))
    ys = _ssd(xbc, dt, cc, cr, proj3, xbc_m, dt_m, cc_m, cr_m, dsk, row(ssd_norm_w[0]),
              _pick(seq, (512, 256, 128)))

    out = _outproj(o.reshape(rows, ATTN_W), proj, ys.reshape(rows, SSD_W), x2,
                   w_attn_proj[0].astype(BF16), w_ssd_proj[0].astype(BF16), w_out[0].astype(BF16),
                   row(final_norm_w), _pick(rows, (512, 256, 128)))
    return out.reshape(bsz, seq, D_MODEL)
```

```python
import functools
import math

import jax
import jax.numpy as jnp
from jax import lax
from jax.experimental import pallas as pl
from jax.experimental.pallas import tpu as pltpu

F32 = jnp.float32
BF16 = jnp.bfloat16

D_MODEL = 1024
N_META = 16
N_HEADS = 16
NOPE = 64
ROPE = 32
QK_DIM = NOPE + ROPE
V_DIM = 64
Q_RANK = 384
KV_RANK = 256
ROPE_THETA = 10000.0
ATTN_W = N_HEADS * V_DIM
SSD_W = 2048
SSD_P = 64
SSD_H = SSD_W // SSD_P
SSD_G = 4
SSD_J = SSD_H // SSD_G
D_STATE = 128
CONV_K = 5
GN = SSD_G * D_STATE
CONV_DIM = SSD_W + 2 * GN
CHUNK = 128
EPS = 1e-6
HP = 128
META_PAD = CHUNK - N_META

C_KV, C_KR, C_Q, C_DT = 0, 256, 384, 768
C_G, C_Z, C_XS, C_MG, C_B, C_C = 1024, 2048, 4096, 6144, 8192, 8704
NP = 9216
O_Q, O_KV, O_KR, O_G, O_Z, O_XBC, O_DT, O_MG = 0, 384, 640, 672, 1696, 3744, 6816, 6880

VMEM_LIMIT = 56 * 1024 * 1024
NEG_INF = float("-inf")
LOG2E = math.log2(math.e)


def _cp(sem):
    return pltpu.CompilerParams(dimension_semantics=sem, vmem_limit_bytes=VMEM_LIMIT)


def _sigmoid(x):
    return 1.0 / (1.0 + jnp.exp(-x))


def _rms(x, w):
    return x * lax.rsqrt(jnp.mean(x * x, axis=-1, keepdims=True) + EPS) * w


def _inproj_kernel(x_ref, nw_ref, w_ref, o_ref, dt_ref, u_ref):
    j = pl.program_id(1)

    @pl.when(j == 0)
    def _():
        u_ref[...] = _rms(x_ref[...], nw_ref[...]).astype(BF16)

    acc = jnp.dot(u_ref[...], w_ref[...], preferred_element_type=F32)
    o_ref[...] = acc.astype(BF16)

    @pl.when(j == 0)
    def _():
        dt_ref[...] = acc[:, C_DT:C_DT + 128]


def _inproj(x2, nw, w, tm, tn=1024):
    rows = x2.shape[0]
    return pl.pallas_call(
        _inproj_kernel,
        out_shape=(jax.ShapeDtypeStruct((rows, NP), BF16),
                   jax.ShapeDtypeStruct((rows, 128), F32)),
        grid=(rows // tm, NP // tn),
        in_specs=[pl.BlockSpec((tm, D_MODEL), lambda i, j: (i, 0)),
                  pl.BlockSpec((1, D_MODEL), lambda i, j: (0, 0)),
                  pl.BlockSpec((D_MODEL, tn), lambda i, j: (0, j))],
        out_specs=(pl.BlockSpec((tm, tn), lambda i, j: (i, j)),
                   pl.BlockSpec((tm, 128), lambda i, j: (i, 0))),
        scratch_shapes=[pltpu.VMEM((tm, D_MODEL), BF16)],
        compiler_params=_cp(("parallel", "arbitrary")),
        name="inproj",
    )(x2, nw, w)


def _qkv_kernel(hd_ref, cos_ref, sp_ref, sm_ref, qnw_ref, kvnw_ref, wq_ref, wkv_ref,
                q_ref, k_ref, v_ref, *, scale):
    cos_t, sin_p, sin_m = cos_ref[...], sp_ref[...], sm_ref[...]

    def rope(t):
        return t * cos_t + pltpu.roll(t, 16, 1) * sin_p + pltpu.roll(t, 112, 1) * sin_m

    kv_lat = hd_ref[:, C_KV:C_KV + KV_RANK].astype(F32)
    kvn = _rms(kv_lat, kvnw_ref[...]).astype(BF16)
    kv = jnp.dot(kvn, wkv_ref[...], preferred_element_type=F32)
    kpe = rope(hd_ref[:, C_KR:C_KR + HP].astype(F32))
    for h in range(N_HEADS):
        sl = slice(h * HP, (h + 1) * HP)
        k_ref[:, sl] = (kv[:, sl] + kpe).astype(BF16)
    vv = kv[:, N_HEADS * HP:]
    lane = lax.broadcasted_iota(jnp.int32, vv.shape, 1)
    v_ref[...] = jnp.where(lane % HP == V_DIM, 1.0, vv).astype(BF16)

    q_lat = hd_ref[:, C_Q:C_Q + Q_RANK].astype(F32)
    qn = _rms(q_lat, qnw_ref[...]).astype(BF16)
    q = jnp.dot(qn, wq_ref[...], preferred_element_type=F32)
    for h in range(N_HEADS):
        sl = slice(h * HP, (h + 1) * HP)
        q_ref[:, sl] = (rope(q[:, sl]) * scale).astype(BF16)


def _qkv(proj, tabs, qnw, kvnw, wq, wkv, tm, tiles_per_seq):
    rows = proj.shape[0]
    w = N_HEADS * HP
    tab_spec = pl.BlockSpec((tm, HP), lambda i: (i % tiles_per_seq, 0))
    out = jax.ShapeDtypeStruct((rows, w), BF16)
    return pl.pallas_call(
        functools.partial(_qkv_kernel, scale=LOG2E * QK_DIM ** -0.5),
        out_shape=(out, out, out),
        grid=(rows // tm,),
        in_specs=[pl.BlockSpec((tm, 1024), lambda i: (i, 0)),
                  tab_spec, tab_spec, tab_spec,
                  pl.BlockSpec((1, Q_RANK), lambda i: (0, 0)),
                  pl.BlockSpec((1, KV_RANK), lambda i: (0, 0)),
                  pl.BlockSpec((Q_RANK, w), lambda i: (0, 0)),
                  pl.BlockSpec((KV_RANK, 2 * w), lambda i: (0, 0))],
        out_specs=(pl.BlockSpec((tm, w), lambda i: (i, 0)),) * 3,
        compiler_params=_cp(("parallel",)),
        name="qkv",
    )(proj, *tabs, qnw, kvnw, wq, wkv)


def _flash_kernel(q_ref, k_ref, v_ref, km_ref, vm_ref, o_ref, m_ref, acc_ref, *, tk, nk):
    nt = (((1,), (1,)), ((), ()))
    heads = [slice(hh * HP, (hh + 1) * HP) for hh in range(2)]
    for hh, sl in enumerate(heads):
        s0 = lax.dot_general(q_ref[:, sl], km_ref[:, sl], nt, preferred_element_type=F32)
        col = lax.broadcasted_iota(jnp.int32, s0.shape, 1)
        s0 = jnp.where(col >= META_PAD, s0, NEG_INF)
        m = jnp.broadcast_to(jnp.max(s0, axis=-1, keepdims=True), s0.shape)
        m_ref[hh] = m
        acc_ref[hh] = jnp.dot(jnp.exp2(s0 - m).astype(BF16), vm_ref[:, sl], preferred_element_type=F32)

    def body(j, carry):
        off = pl.multiple_of(j * tk, tk)
        ss = [lax.dot_general(q_ref[:, sl], k_ref[pl.ds(off, tk), sl], nt, preferred_element_type=F32)
              for sl in heads]
        for hh, sl in enumerate(heads):
            blocks = [ss[hh][:, c * HP:(c + 1) * HP] for c in range(tk // HP)]
            m = m_ref[hh]
            m_new = jnp.maximum(m, jnp.max(functools.reduce(jnp.maximum, blocks), axis=-1, keepdims=True))
            p = jnp.concatenate([jnp.exp2(blk - m_new) for blk in blocks], axis=-1).astype(BF16)
            pv = jnp.dot(p, v_ref[pl.ds(off, tk), sl], preferred_element_type=F32)
            acc_ref[hh] = jnp.exp2(m - m_new) * acc_ref[hh] + pv
            m_ref[hh] = m_new
        return carry

    lax.fori_loop(0, nk, body, 0, unroll=4)
    outs = [acc_ref[hh][:, :V_DIM] / acc_ref[hh][:, V_DIM:V_DIM + 1] for hh in range(2)]
    o_ref[...] = jnp.concatenate(outs, axis=-1).astype(BF16)


def _flash(q, k, v, km, vm, tq, tk):
    bsz, seq, _ = q.shape
    return pl.pallas_call(
        functools.partial(_flash_kernel, tk=tk, nk=seq // tk),
        out_shape=jax.ShapeDtypeStruct((bsz, seq, ATTN_W), BF16),
        grid=(bsz, N_HEADS // 2, seq // tq),
        in_specs=[pl.BlockSpec((None, tq, 2 * HP), lambda b, h, i: (b, i, h)),
                  pl.BlockSpec((None, seq, 2 * HP), lambda b, h, i: (b, 0, h)),
                  pl.BlockSpec((None, seq, 2 * HP), lambda b, h, i: (b, 0, h)),
                  pl.BlockSpec((CHUNK, 2 * HP), lambda b, h, i: (0, h)),
                  pl.BlockSpec((CHUNK, 2 * HP), lambda b, h, i: (0, h))],
        out_specs=pl.BlockSpec((None, tq, 2 * V_DIM), lambda b, h, i: (b, i, h)),
        scratch_shapes=[pltpu.VMEM((2, tq, HP), F32), pltpu.VMEM((2, tq, HP), F32)],
        compiler_params=_cp(("parallel", "parallel", "arbitrary")),
        name="flash",
    )(q, k, v, km, vm)


def _conv_taps(ext_ref, w_ref, b_ref, rows):
    ext = ext_ref[...]
    n = ext.shape[0]
    acc = b_ref[...] + w_ref[CONV_K // 2:CONV_K // 2 + 1, :] * ext[16:16 + rows]
    for t in range(CONV_K):
        if t != CONV_K // 2:
            shifted = pltpu.roll(ext, (CONV_K // 2 - t) % n, 0)
            acc = acc + w_ref[t:t + 1, :] * shifted[16:16 + rows]
    return acc * _sigmoid(acc)


def _conv_kernel(main_ref, prev_ref, next_ref, meta_ref, w_ref, b_ref, o_ref, ext_ref, *, tr, nr):
    r = pl.program_id(2)
    prev = jnp.where(r == 0, meta_ref[CHUNK - 16:CHUNK, :], prev_ref[...])
    nxt = jnp.where(r == nr - 1, jnp.zeros_like(next_ref[...]), next_ref[...])
    ext_ref[0:16, :] = prev.astype(F32)
    ext_ref[16:16 + tr, :] = main_ref[...].astype(F32)
    ext_ref[16 + tr:32 + tr, :] = nxt.astype(F32)
    o_ref[...] = _conv_taps(ext_ref, w_ref, b_ref, tr).astype(BF16)


def _conv_col(c):
    return jnp.where(c < 4, C_XS // 512 + c, C_B // 512 + (c - 4))


def _conv(proj3, proj_meta, cw, cb, tr):
    bsz, seq, _ = proj3.shape
    nr = seq // tr
    hb = tr // 16
    return pl.pallas_call(
        functools.partial(_conv_kernel, tr=tr, nr=nr),
        out_shape=jax.ShapeDtypeStruct((bsz, seq, CONV_DIM), BF16),
        grid=(bsz, CONV_DIM // 512, nr),
        in_specs=[pl.BlockSpec((None, tr, 512), lambda b, c, r: (b, r, _conv_col(c))),
                  pl.BlockSpec((None, 16, 512),
                               lambda b, c, r: (b, jnp.maximum(r * hb - 1, 0), _conv_col(c))),
                  pl.BlockSpec((None, 16, 512),
                               lambda b, c, r: (b, jnp.minimum((r + 1) * hb, seq // 16 - 1), _conv_col(c))),
                  pl.BlockSpec((CHUNK, 512), lambda b, c, r: (0, _conv_col(c))),
                  pl.BlockSpec((CONV_K, 512), lambda b, c, r: (0, c)),
                  pl.BlockSpec((1, 512), lambda b, c, r: (0, c))],
        out_specs=pl.BlockSpec((None, tr, 512), lambda b, c, r: (b, r, c)),
        scratch_shapes=[pltpu.VMEM((tr + 32, 512), F32)],
        compiler_params=_cp(("parallel", "parallel", "arbitrary")),
        name="conv",
    )(proj3, proj3, proj3, proj_meta, cw, cb)


def _conv_meta_kernel(meta_ref, main_ref, w_ref, b_ref, o_ref, ext_ref):
    ext_ref[0:32, :] = meta_ref[CHUNK - 32:CHUNK, :].astype(F32)
    ext_ref[32:48, :] = main_ref[...].astype(F32)
    o_ref[0:META_PAD, :] = jnp.zeros((META_PAD, 512), BF16)
    o_ref[META_PAD:CHUNK, :] = _conv_taps(ext_ref, w_ref, b_ref, N_META).astype(BF16)


def _conv_meta(proj3, proj_meta, cw, cb):
    bsz = proj3.shape[0]
    return pl.pallas_call(
        _conv_meta_kernel,
        out_shape=jax.ShapeDtypeStruct((bsz, CHUNK, CONV_DIM), BF16),
        grid=(bsz, CONV_DIM // 512),
        in_specs=[pl.BlockSpec((CHUNK, 512), lambda b, c: (0, _conv_col(c))),
                  pl.BlockSpec((None, 16, 512), lambda b, c: (b, 0, _conv_col(c))),
                  pl.BlockSpec((CONV_K, 512), lambda b, c: (0, c)),
                  pl.BlockSpec((1, 512), lambda b, c: (0, c))],
        out_specs=pl.BlockSpec((None, CHUNK, 512), lambda b, c: (b, 0, c)),
        scratch_shapes=[pltpu.VMEM((48, 512), F32)],
        compiler_params=_cp(("parallel", "parallel")),
        name="conv_meta",
    )(proj_meta, proj3, cw, cb)


def _dt_kernel(raw_ref, bias_ref, alog_ref, dt_ref, cc_ref, cr_ref, *, cpb, valid_from):
    row = lax.broadcasted_iota(jnp.int32, (CHUNK, 128), 0)
    col = lax.broadcasted_iota(jnp.int32, (CHUNK, 128), 1)
    tri = (col <= row).astype(F32)
    triu = (col >= row).astype(F32)
    a = -jnp.exp(alog_ref[...]) * LOG2E
    for c in range(cpb):
        rs = slice(c * CHUNK, (c + 1) * CHUNK)
        x = raw_ref[rs, :] + bias_ref[...]
        dt = jnp.maximum(x, 0.0) + jnp.log(1.0 + jnp.exp(-jnp.abs(x)))
        dt = jnp.where(col < 2 * SSD_H, dt, 0.0)
        if valid_from:
            dt = jnp.where(row >= valid_from, dt, 0.0)
        da = dt * a
        cf = jnp.dot(tri, da, precision=lax.Precision.HIGHEST, preferred_element_type=F32)
        cb = jnp.dot(triu, da, precision=lax.Precision.HIGHEST, preferred_element_type=F32)
        cum = jnp.where(col < SSD_H, cf, cb)
        dt_ref[rs, :] = dt
        cc_ref[rs, :] = cum
        cr_ref[c] = cum.T


def _dtprep(raw, bias, alog, cpb, valid_from):
    rows = raw.shape[0]
    nch = rows // CHUNK
    blk = cpb * CHUNK
    o = jax.ShapeDtypeStruct((rows, 128), F32)
    return pl.pallas_call(
        functools.partial(_dt_kernel, cpb=cpb, valid_from=valid_from),
        out_shape=(o, o, jax.ShapeDtypeStruct((nch, 128, CHUNK), F32)),
        grid=(nch // cpb,),
        in_specs=[pl.BlockSpec((blk, 128), lambda i: (i, 0)),
                  pl.BlockSpec((1, 128), lambda i: (0, 0)),
                  pl.BlockSpec((1, 128), lambda i: (0, 0))],
        out_specs=(pl.BlockSpec((blk, 128), lambda i: (i, 0)),
                   pl.BlockSpec((blk, 128), lambda i: (i, 0)),
                   pl.BlockSpec((cpb, 128, CHUNK), lambda i: (i, 0, 0))),
        compiler_params=_cp(("parallel",)),
        name="dtprep",
    )(raw, bias, alog)


def _ssd_chunk(state, xbc, dtc, cc, cr, g, fwd, want_y):
    row = lax.broadcasted_iota(jnp.int32, (CHUNK, CHUNK), 0)
    col = lax.broadcasted_iota(jnp.int32, (CHUNK, CHUNK), 1)
    keep = (col <= row) if fwd else (col >= row)
    lo = col < SSD_P
    hb = (0 if fwd else SSD_H) + g * SSD_J
    edge = CHUNK - 1 if fwd else 0

    xs = xbc[:, g * 512:(g + 1) * 512].astype(F32)
    bm = xbc[:, SSD_W + g * D_STATE:SSD_W + (g + 1) * D_STATE]
    cm = xbc[:, SSD_W + GN + g * D_STATE:SSD_W + GN + (g + 1) * D_STATE]
    bt = bm.astype(F32).T.astype(BF16)
    if want_y:
        cbm = lax.dot_general(cm, bm, (((1,), (1,)), ((), ())), preferred_element_type=F32)
        yoff = jnp.dot(cm, state.astype(BF16), preferred_element_type=F32)

    def bcl(a, j):
        return jnp.broadcast_to(a[:, hb + j:hb + j + 1], (CHUNK, CHUNK))

    ys, xws, decs = [], [], []
    for kk in range(SSD_J // 2):
        j0, j1 = 2 * kk, 2 * kk + 1
        cc0, cc1 = bcl(cc, j0), bcl(cc, j1)
        cc_pair = jnp.where(lo, cc0, cc1)
        dt_pair = jnp.take_along_axis(dtc, jnp.where(lo, hb + j0, hb + j1), axis=1)
        xdt = xs[:, kk * 128:(kk + 1) * 128] * dt_pair
        tot = cc_pair[edge:edge + 1, :]
        xws.append((xdt * jnp.exp2(tot - cc_pair)).astype(BF16))
        decs.append(jnp.exp2(tot))
        if want_y:
            xdt_b = xdt.astype(BF16)

            def ydiag(ccj, j):
                crj = jnp.broadcast_to(cr[hb + j:hb + j + 1, :], (CHUNK, CHUNK))
                decay = jnp.exp2(jnp.where(keep, ccj - crj, NEG_INF))
                return jnp.dot((cbm * decay).astype(BF16), xdt_b, preferred_element_type=F32)

            yd = jnp.where(lo, ydiag(cc0, j0), ydiag(cc1, j1))
            ys.append(yd + yoff[:, kk * 128:(kk + 1) * 128] * jnp.exp2(cc_pair))
    xw = jnp.concatenate(xws, axis=1)
    dec = jnp.concatenate(decs, axis=1)
    new_state = state * dec + jnp.dot(bt, xw, preferred_element_type=F32)
    return new_state, (jnp.concatenate(ys, axis=1) if want_y else None)


def _ssd_bwd_kernel(xbc_ref, dt_ref, cc_ref, cr_ref, yb_ref, st_ref, *, ncs):
    s = pl.program_id(1)

    @pl.when(s == 0)
    def _():
        st_ref[...] = jnp.zeros_like(st_ref)

    def body(i, carry):
        c = ncs - 1 - i
        r0 = pl.multiple_of(c * CHUNK, CHUNK)
        rs = pl.ds(r0, CHUNK)
        for g in range(SSD_G):
            st, y = _ssd_chunk(st_ref[g], xbc_ref[rs, :], dt_ref[rs, :], cc_ref[rs, :], cr_ref[c],
                               g, False, True)
            st_ref[g] = st
            yb_ref[rs, g * 512:(g + 1) * 512] = y.astype(BF16)
        return carry

    lax.fori_loop(0, ncs, body, 0)


def _ssd_fwd_kernel(xbc_ref, dt_ref, cc_ref, cr_ref, z_ref, yb_ref, xm_ref, dtm_ref, ccm_ref, crm_ref,
                    dsk_ref, nw_ref, o_ref, st_ref, *, ncs):
    s = pl.program_id(1)

    @pl.when(s == 0)
    def _():
        for g in range(SSD_G):
            st, _ = _ssd_chunk(jnp.zeros((D_STATE, 512), F32), xm_ref[...], dtm_ref[...], ccm_ref[...],
                               crm_ref[0], g, True, False)
            st_ref[g] = st

    def body(c, carry):
        r0 = pl.multiple_of(c * CHUNK, CHUNK)
        rs = pl.ds(r0, CHUNK)
        for g in range(SSD_G):
            gs = slice(g * 512, (g + 1) * 512)
            xbc = xbc_ref[rs, :]
            st, y = _ssd_chunk(st_ref[g], xbc, dt_ref[rs, :], cc_ref[rs, :], cr_ref[c], g, True, True)
            st_ref[g] = st
            y = y + yb_ref[rs, gs].astype(F32) + dsk_ref[:, gs] * xbc[:, gs].astype(F32)
            z = z_ref[rs, gs].astype(F32)
            y = y * (z * _sigmoid(z))
            o_ref[rs, gs] = _rms(y, nw_ref[:, gs]).astype(BF16)
        return carry

    lax.fori_loop(0, ncs, body, 0)


def _ssd(xbc, dt, cc, cr, proj3, xbc_m, dt_m, cc_m, cr_m, dsk, nw, ts):
    bsz, seq, _ = xbc.shape
    ncs = ts // CHUNK
    nseg = seq // ts
    dt3 = dt.reshape(bsz, seq, 128)
    cc3 = cc.reshape(bsz, seq, 128)
    cr4 = cr.reshape(bsz, seq // CHUNK, 128, CHUNK)
    st = pltpu.VMEM((SSD_G, D_STATE, 512), F32)

    def seg_specs(rev):
        ix = (lambda s: nseg - 1 - s) if rev else (lambda s: s)
        return [pl.BlockSpec((None, ts, CONV_DIM), lambda b, s: (b, ix(s), 0)),
                pl.BlockSpec((None, ts, 128), lambda b, s: (b, ix(s), 0)),
                pl.BlockSpec((None, ts, 128), lambda b, s: (b, ix(s), 0)),
                pl.BlockSpec((None, ncs, 128, CHUNK), lambda b, s: (b, ix(s), 0, 0))]

    yb = pl.pallas_call(
        functools.partial(_ssd_bwd_kernel, ncs=ncs),
        out_shape=jax.ShapeDtypeStruct((bsz, seq, SSD_W), BF16),
        grid=(bsz, nseg),
        in_specs=seg_specs(True),
        out_specs=pl.BlockSpec((None, ts, SSD_W), lambda b, s: (b, nseg - 1 - s, 0)),
        scratch_shapes=[st],
        compiler_params=_cp(("parallel", "arbitrary")),
        name="ssd_bwd",
    )(xbc, dt3, cc3, cr4)

    return pl.pallas_call(
        functools.partial(_ssd_fwd_kernel, ncs=ncs),
        out_shape=jax.ShapeDtypeStruct((bsz, seq, SSD_W), BF16),
        grid=(bsz, nseg),
        in_specs=seg_specs(False) + [
            pl.BlockSpec((None, ts, SSD_W), lambda b, s: (b, s, C_Z // SSD_W)),
            pl.BlockSpec((None, ts, SSD_W), lambda b, s: (b, s, 0)),
            pl.BlockSpec((None, CHUNK, CONV_DIM), lambda b, s: (b, 0, 0)),
            pl.BlockSpec((CHUNK, 128), lambda b, s: (0, 0)),
            pl.BlockSpec((CHUNK, 128), lambda b, s: (0, 0)),
            pl.BlockSpec((1, 128, CHUNK), lambda b, s: (0, 0, 0)),
            pl.BlockSpec((1, SSD_W), lambda b, s: (0, 0)),
            pl.BlockSpec((1, SSD_W), lambda b, s: (0, 0))],
        out_specs=pl.BlockSpec((None, ts, SSD_W), lambda b, s: (b, s, 0)),
        scratch_shapes=[st],
        compiler_params=_cp(("parallel", "arbitrary")),
        name="ssd_fwd",
    )(xbc, dt3, cc3, cr4, proj3, yb, xbc_m, dt_m, cc_m, cr_m, dsk, nw)


def _out_kernel(o_ref, g_ref, ys_ref, mg_ref, x_ref, wa_ref, ws_ref, wo_ref, fnw_ref, out_ref):
    g = g_ref[...].astype(F32)
    ag = (o_ref[...].astype(F32) * (g * _sigmoid(g))).astype(BF16)
    ya = jnp.dot(ag, wa_ref[...], preferred_element_type=F32)
    ys = jnp.dot(ys_ref[...], ws_ref[...], preferred_element_type=F32)
    gate = _sigmoid(mg_ref[...].astype(F32))
    mixed = (gate[:, :D_MODEL] * ya + gate[:, D_MODEL:] * ys).astype(BF16)
    h = x_ref[...] + jnp.dot(mixed, wo_ref[...], preferred_element_type=F32)
    out_ref[...] = _rms(h, fnw_ref[...])


def _outproj(o2, proj, ys2, x2, wa, ws, wo, fnw, tm):
    rows = x2.shape[0]
    full = lambda shape: pl.BlockSpec(shape, lambda i: (0, 0))
    return pl.pallas_call(
        _out_kernel,
        out_shape=jax.ShapeDtypeStruct((rows, D_MODEL), F32),
        grid=(rows // tm,),
        in_specs=[pl.BlockSpec((tm, ATTN_W), lambda i: (i, 0)),
                  pl.BlockSpec((tm, ATTN_W), lambda i: (i, C_G // ATTN_W)),
                  pl.BlockSpec((tm, SSD_W), lambda i: (i, 0)),
                  pl.BlockSpec((tm, 2 * D_MODEL), lambda i: (i, C_MG // (2 * D_MODEL))),
                  pl.BlockSpec((tm, D_MODEL), lambda i: (i, 0)),
                  full((ATTN_W, D_MODEL)), full((SSD_W, D_MODEL)), full((D_MODEL, D_MODEL)),
                  full((1, D_MODEL))],
        out_specs=pl.BlockSpec((tm, D_MODEL), lambda i: (i, 0)),
        compiler_params=_cp(("parallel",)),
        name="outproj",
    )(o2, proj, ys2, proj, x2, wa, ws, wo, fnw)


def _rope_tables(pos):
    inv_freq = ROPE_THETA ** (-jnp.arange(0, ROPE, 2, dtype=F32) / ROPE)
    ang = pos.astype(F32)[:, None] * inv_freq[None, :]
    cos, sin = jnp.cos(ang), jnp.sin(ang)
    n = pos.shape[0]
    one = jnp.ones((n, NOPE), F32)
    z16 = jnp.zeros((n, ROPE // 2), F32)
    z32 = jnp.zeros((n, HP - QK_DIM), F32)
    z64 = jnp.zeros((n, NOPE), F32)
    cos_t = jnp.concatenate([one, cos, cos, z32 + 1.0], axis=1)
    sin_p = jnp.concatenate([z64, z16, sin, z32], axis=1)
    sin_m = jnp.concatenate([z64, -sin, z16, z32], axis=1)
    return cos_t, sin_p, sin_m


def _pick(n, prefs):
    for t in prefs:
        if n % t == 0:
            return t
    raise ValueError(f"no tile for {n}")


def kernel(x, meta_tokens, norm_w, w_in, q_norm_w, w_uq, kv_norm_w, w_ukv, w_attn_proj, conv_w, conv_b,
           dt_bias, a_log, d_skip, ssd_norm_w, w_ssd_proj, w_out, final_norm_w):
    bsz, seq, d = x.shape
    assert d == D_MODEL and seq % 128 == 0 and norm_w.shape[0] == 1
    rows = bsz * seq
    zc = lambda n: jnp.zeros((D_MODEL, n), F32)
    w = w_in[0]
    w_p = jnp.concatenate([
        w[:, O_KV:O_KV + KV_RANK], zc(NOPE), w[:, O_KR:O_KR + ROPE], zc(HP - QK_DIM),
        w[:, O_Q:O_Q + Q_RANK], w[:, O_DT:O_DT + 2 * SSD_H], zc(128 - 2 * SSD_H), zc(128),
        w[:, O_G:O_G + ATTN_W], w[:, O_Z:O_Z + SSD_W], w[:, O_XBC:O_XBC + SSD_W],
        w[:, O_MG:O_MG + 2 * D_MODEL], w[:, O_XBC + SSD_W:O_XBC + CONV_DIM]], axis=1).astype(BF16)
    assert w_p.shape[1] == NP
    pad_h = lambda t: jnp.pad(t, ((0, 0), (0, 0), (0, HP - t.shape[-1]))).reshape(t.shape[0], N_HEADS * HP)
    wq = pad_h(w_uq[0].reshape(Q_RANK, N_HEADS, QK_DIM)).astype(BF16)
    ukv = w_ukv[0].reshape(KV_RANK, N_HEADS, NOPE + V_DIM)
    wkv = jnp.concatenate([pad_h(ukv[..., :NOPE]), pad_h(ukv[..., NOPE:])], axis=1).astype(BF16)
    row = lambda t: t.reshape(1, -1)
    pad128 = lambda t: jnp.pad(t.reshape(1, -1), ((0, 0), (0, 128 - t.size)))

    tm = _pick(rows, (2048, 1024, 512, 256, 128))
    x2 = x.reshape(rows, D_MODEL)
    proj, dt_raw = _inproj(x2, row(norm_w[0]), w_p, tm)
    xm = jnp.concatenate([jnp.zeros((META_PAD, D_MODEL), F32), meta_tokens.astype(F32)], axis=0)
    proj_m, dt_raw_m = _inproj(xm, row(norm_w[0]), w_p, CHUNK)

    tabs = _rope_tables(N_META + jnp.arange(seq))
    tabs_m = _rope_tables(jnp.maximum(jnp.arange(CHUNK) - META_PAD, 0))
    tq2 = _pick(seq, (512, 256, 128))
    q, k, v = _qkv(proj, tabs, row(q_norm_w[0]), row(kv_norm_w[0]), wq, wkv, tq2, seq // tq2)
    _, km, vm = _qkv(proj_m, tabs_m, row(q_norm_w[0]), row(kv_norm_w[0]), wq, wkv, CHUNK, 1)

    r3 = lambda t: t.reshape(bsz, seq, t.shape[-1])
    tq = _pick(seq, (2048, 1024, 512, 256, 128))
    tk = _pick(seq, (512, 256, 128))
    o = _flash(r3(q), r3(k), r3(v), km, vm, tq, tk)

    proj3 = r3(proj)
    cw, cb = conv_w[0], row(conv_b[0])
    xbc = _conv(proj3, proj_m, cw, cb, _pick(seq, (2048, 1024, 512, 256, 128)))
    xbc_m = _conv_meta(proj3, proj_m, cw, cb)
    bias, alog = pad128(dt_bias[0]), pad128(a_log[0])
    dt, cc, cr = _dtprep(dt_raw, bias, alog, _pick(rows // CHUNK, (8, 4, 2, 1)), 0)
    dt_m, cc_m, cr_m = _dtprep(dt_raw_m, bias, alog, 1, META_PAD)
    dsk = row(jnp.repeat(d_skip[0], SSD_P))
    ys = _ssd(xbc, dt, cc, cr, proj3, xbc_m, dt_m, cc_m, cr_m, dsk, row(ssd_norm_w[0]),
              _pick(seq, (1024, 512, 256, 128)))

    out = _outproj(o.reshape(rows, ATTN_W), proj, ys.reshape(rows, SSD_W), x2,
                   w_attn_proj[0].astype(BF16), w_ssd_proj[0].astype(BF16), w_out[0].astype(BF16),
                   row(final_norm_w), _pick(rows, (512, 256, 128)))
    return out.reshape(bsz, seq, D_MODEL)
```

```python
import functools
import math

import jax
import jax.numpy as jnp
from jax import lax
from jax.experimental import pallas as pl
from jax.experimental.pallas import tpu as pltpu

F32 = jnp.float32
BF16 = jnp.bfloat16

D_MODEL = 1024
N_META = 16
N_HEADS = 16
NOPE = 64
ROPE = 32
QK_DIM = NOPE + ROPE
V_DIM = 64
Q_RANK = 384
KV_RANK = 256
ROPE_THETA = 10000.0
ATTN_W = N_HEADS * V_DIM
SSD_W = 2048
SSD_P = 64
SSD_H = SSD_W // SSD_P
SSD_G = 4
SSD_J = SSD_H // SSD_G
D_STATE = 128
CONV_K = 5
GN = SSD_G * D_STATE
CONV_DIM = SSD_W + 2 * GN
CHUNK = 128
EPS = 1e-6
HP = 128
META_PAD = CHUNK - N_META

C_KV, C_KR, C_Q, C_DT = 0, 256, 384, 768
C_G, C_Z, C_XS, C_MG, C_B, C_C = 1024, 2048, 4096, 6144, 8192, 8704
NP = 9216
O_Q, O_KV, O_KR, O_G, O_Z, O_XBC, O_DT, O_MG = 0, 384, 640, 672, 1696, 3744, 6816, 6880

VMEM_LIMIT = 56 * 1024 * 1024
NEG_INF = float("-inf")
LOG2E = math.log2(math.e)


def _cp(sem):
    return pltpu.CompilerParams(dimension_semantics=sem, vmem_limit_bytes=VMEM_LIMIT)


def _sigmoid(x):
    return 1.0 / (1.0 + jnp.exp(-x))


def _rms(x, w):
    return x * lax.rsqrt(jnp.mean(x * x, axis=-1, keepdims=True) + EPS) * w


def _inproj_kernel(x_ref, nw_ref, w_ref, o_ref, dt_ref, u_ref):
    j = pl.program_id(1)

    @pl.when(j == 0)
    def _():
        u_ref[...] = _rms(x_ref[...], nw_ref[...]).astype(BF16)

    acc = jnp.dot(u_ref[...], w_ref[...], preferred_element_type=F32)
    o_ref[...] = acc.astype(BF16)

    @pl.when(j == 0)
    def _():
        dt_ref[...] = acc[:, C_DT:C_DT + 128]


def _inproj(x2, nw, w, tm, tn=1024):
    rows = x2.shape[0]
    return pl.pallas_call(
        _inproj_kernel,
        out_shape=(jax.ShapeDtypeStruct((rows, NP), BF16),
                   jax.ShapeDtypeStruct((rows, 128), F32)),
        grid=(rows // tm, NP // tn),
        in_specs=[pl.BlockSpec((tm, D_MODEL), lambda i, j: (i, 0)),
                  pl.BlockSpec((1, D_MODEL), lambda i, j: (0, 0)),
                  pl.BlockSpec((D_MODEL, tn), lambda i, j: (0, j))],
        out_specs=(pl.BlockSpec((tm, tn), lambda i, j: (i, j)),
                   pl.BlockSpec((tm, 128), lambda i, j: (i, 0))),
        scratch_shapes=[pltpu.VMEM((tm, D_MODEL), BF16)],
        compiler_params=_cp(("parallel", "arbitrary")),
        name="inproj",
    )(x2, nw, w)


def _qkv_kernel(hd_ref, cos_ref, sp_ref, sm_ref, qnw_ref, kvnw_ref, wq_ref, wkv_ref,
                q_ref, k_ref, v_ref, *, scale):
    cos_t, sin_p, sin_m = cos_ref[...], sp_ref[...], sm_ref[...]

    def rope(t):
        return t * cos_t + pltpu.roll(t, 16, 1) * sin_p + pltpu.roll(t, 112, 1) * sin_m

    kv_lat = hd_ref[:, C_KV:C_KV + KV_RANK].astype(F32)
    kvn = _rms(kv_lat, kvnw_ref[...]).astype(BF16)
    kv = jnp.dot(kvn, wkv_ref[...], preferred_element_type=F32)
    kpe = rope(hd_ref[:, C_KR:C_KR + HP].astype(F32))
    lane = lax.broadcasted_iota(jnp.int32, kpe.shape, 1)
    for h in range(N_HEADS):
        sl = slice(h * HP, (h + 1) * HP)
        k_ref[:, sl] = jnp.where(lane < NOPE, kv[:, sl], kpe).astype(BF16)
        v_ref[:, sl] = jnp.where(lane < NOPE, 1.0, kv[:, sl]).astype(BF16)

    q_lat = hd_ref[:, C_Q:C_Q + Q_RANK].astype(F32)
    qn = _rms(q_lat, qnw_ref[...]).astype(BF16)
    q = jnp.dot(qn, wq_ref[...], preferred_element_type=F32)
    for h in range(N_HEADS):
        sl = slice(h * HP, (h + 1) * HP)
        q_ref[:, sl] = (rope(q[:, sl]) * scale).astype(BF16)


def _qkv(proj, tabs, qnw, kvnw, wq, wkv, tm, tiles_per_seq):
    rows = proj.shape[0]
    w = N_HEADS * HP
    tab_spec = pl.BlockSpec((tm, HP), lambda i: (i % tiles_per_seq, 0))
    out = jax.ShapeDtypeStruct((rows, w), BF16)
    return pl.pallas_call(
        functools.partial(_qkv_kernel, scale=LOG2E * QK_DIM ** -0.5),
        out_shape=(out, out, out),
        grid=(rows // tm,),
        in_specs=[pl.BlockSpec((tm, 1024), lambda i: (i, 0)),
                  tab_spec, tab_spec, tab_spec,
                  pl.BlockSpec((1, Q_RANK), lambda i: (0, 0)),
                  pl.BlockSpec((1, KV_RANK), lambda i: (0, 0)),
                  pl.BlockSpec((Q_RANK, w), lambda i: (0, 0)),
                  pl.BlockSpec((KV_RANK, w), lambda i: (0, 0))],
        out_specs=(pl.BlockSpec((tm, w), lambda i: (i, 0)),) * 3,
        compiler_params=_cp(("parallel",)),
        name="qkv",
    )(proj, *tabs, qnw, kvnw, wq, wkv)


def _flash_kernel(q_ref, k_ref, v_ref, km_ref, vm_ref, o_ref, m_ref, acc_ref, *, tk, nk):
    nt = (((1,), (1,)), ((), ()))
    heads = [slice(hh * HP, (hh + 1) * HP) for hh in range(2)]
    s0 = lax.dot_general(q_ref[...], km_ref[...], nt, preferred_element_type=F32)
    col = lax.broadcasted_iota(jnp.int32, s0.shape, 1)
    s0 = jnp.where(col < 2 * N_META, s0, NEG_INF)
    m0 = jnp.max(jnp.where(col < N_META, s0, NEG_INF), axis=-1, keepdims=True)
    m1 = jnp.max(jnp.where(col >= N_META, s0, NEG_INF), axis=-1, keepdims=True)
    p0 = jnp.exp2(s0 - jnp.where(col < N_META, m0, m1)).astype(BF16)
    acc0 = jnp.dot(p0, vm_ref[...], preferred_element_type=F32)
    for hh, (sl, m) in enumerate(zip(heads, (m0, m1))):
        m_ref[hh] = jnp.broadcast_to(m, s0.shape)
        acc_ref[hh] = acc0[:, sl]

    def body(j, carry):
        off = pl.multiple_of(j * tk, tk)
        ss = [lax.dot_general(q_ref[:, sl], k_ref[pl.ds(off, tk), sl], nt, preferred_element_type=F32)
              for sl in heads]
        for hh, sl in enumerate(heads):
            blocks = [ss[hh][:, c * HP:(c + 1) * HP] for c in range(tk // HP)]
            m = m_ref[hh]
            m_new = jnp.maximum(m, jnp.max(functools.reduce(jnp.maximum, blocks), axis=-1, keepdims=True))
            p = jnp.concatenate([jnp.exp2(blk - m_new) for blk in blocks], axis=-1).astype(BF16)
            pv = jnp.dot(p, v_ref[pl.ds(off, tk), sl], preferred_element_type=F32)
            acc_ref[hh] = jnp.exp2(m - m_new) * acc_ref[hh] + pv
            m_ref[hh] = m_new
        return carry

    lax.fori_loop(0, nk, body, 0, unroll=4)
    a0, a1 = acc_ref[0], acc_ref[1]
    lane = lax.broadcasted_iota(jnp.int32, a0.shape, 1)
    o_ref[...] = jnp.where(lane < V_DIM, pltpu.roll(a0, V_DIM, 1) / a0, a1 / pltpu.roll(a1, V_DIM, 1)).astype(BF16)


def _flash(q, k, v, km, vm, tq, tk):
    bsz, seq, _ = q.shape
    return pl.pallas_call(
        functools.partial(_flash_kernel, tk=tk, nk=seq // tk),
        out_shape=jax.ShapeDtypeStruct((bsz, seq, ATTN_W), BF16),
        grid=(bsz, N_HEADS // 2, seq // tq),
        in_specs=[pl.BlockSpec((None, tq, 2 * HP), lambda b, h, i: (b, i, h)),
                  pl.BlockSpec((None, seq, 2 * HP), lambda b, h, i: (b, 0, h)),
                  pl.BlockSpec((None, seq, 2 * HP), lambda b, h, i: (b, 0, h)),
                  pl.BlockSpec((None, CHUNK, 2 * HP), lambda b, h, i: (h, 0, 0)),
                  pl.BlockSpec((None, CHUNK, 2 * HP), lambda b, h, i: (h, 0, 0))],
        out_specs=pl.BlockSpec((None, tq, 2 * V_DIM), lambda b, h, i: (b, i, h)),
        scratch_shapes=[pltpu.VMEM((2, tq, HP), F32), pltpu.VMEM((2, tq, HP), F32)],
        compiler_params=_cp(("parallel", "parallel", "arbitrary")),
        name="flash",
    )(q, k, v, km, vm)


def _conv_taps(ext_ref, w_ref, b_ref, rows):
    ext = ext_ref[...]
    n = ext.shape[0]
    acc = b_ref[...] + w_ref[CONV_K // 2:CONV_K // 2 + 1, :] * ext[16:16 + rows]
    for t in range(CONV_K):
        if t != CONV_K // 2:
            shifted = pltpu.roll(ext, (CONV_K // 2 - t) % n, 0)
            acc = acc + w_ref[t:t + 1, :] * shifted[16:16 + rows]
    return acc * _sigmoid(acc)


def _conv_kernel(main_ref, prev_ref, next_ref, meta_ref, w_ref, b_ref, o_ref, ext_ref, *, tr, nr):
    r = pl.program_id(2)
    prev = jnp.where(r == 0, meta_ref[CHUNK - 16:CHUNK, :], prev_ref[...])
    nxt = jnp.where(r == nr - 1, jnp.zeros_like(next_ref[...]), next_ref[...])
    ext_ref[0:16, :] = prev.astype(F32)
    ext_ref[16:16 + tr, :] = main_ref[...].astype(F32)
    ext_ref[16 + tr:32 + tr, :] = nxt.astype(F32)
    o_ref[...] = _conv_taps(ext_ref, w_ref, b_ref, tr).astype(BF16)


def _conv_col(c):
    return jnp.where(c < 4, C_XS // 512 + c, C_B // 512 + (c - 4))


def _conv(proj3, proj_meta, cw, cb, tr):
    bsz, seq, _ = proj3.shape
    nr = seq // tr
    hb = tr // 16
    return pl.pallas_call(
        functools.partial(_conv_kernel, tr=tr, nr=nr),
        out_shape=jax.ShapeDtypeStruct((bsz, seq, CONV_DIM), BF16),
        grid=(bsz, CONV_DIM // 512, nr),
        in_specs=[pl.BlockSpec((None, tr, 512), lambda b, c, r: (b, r, _conv_col(c))),
                  pl.BlockSpec((None, 16, 512),
                               lambda b, c, r: (b, jnp.maximum(r * hb - 1, 0), _conv_col(c))),
                  pl.BlockSpec((None, 16, 512),
                               lambda b, c, r: (b, jnp.minimum((r + 1) * hb, seq // 16 - 1), _conv_col(c))),
                  pl.BlockSpec((CHUNK, 512), lambda b, c, r: (0, _conv_col(c))),
                  pl.BlockSpec((CONV_K, 512), lambda b, c, r: (0, c)),
                  pl.BlockSpec((1, 512), lambda b, c, r: (0, c))],
        out_specs=pl.BlockSpec((None, tr, 512), lambda b, c, r: (b, r, c)),
        scratch_shapes=[pltpu.VMEM((tr + 32, 512), F32)],
        compiler_params=_cp(("parallel", "parallel", "arbitrary")),
        name="conv",
    )(proj3, proj3, proj3, proj_meta, cw, cb)


def _conv_meta_kernel(meta_ref, main_ref, w_ref, b_ref, o_ref, ext_ref):
    ext_ref[0:32, :] = meta_ref[CHUNK - 32:CHUNK, :].astype(F32)
    ext_ref[32:48, :] = main_ref[...].astype(F32)
    o_ref[0:META_PAD, :] = jnp.zeros((META_PAD, 512), BF16)
    o_ref[META_PAD:CHUNK, :] = _conv_taps(ext_ref, w_ref, b_ref, N_META).astype(BF16)


def _conv_meta(proj3, proj_meta, cw, cb):
    bsz = proj3.shape[0]
    return pl.pallas_call(
        _conv_meta_kernel,
        out_shape=jax.ShapeDtypeStruct((bsz, CHUNK, CONV_DIM), BF16),
        grid=(bsz, CONV_DIM // 512),
        in_specs=[pl.BlockSpec((CHUNK, 512), lambda b, c: (0, _conv_col(c))),
                  pl.BlockSpec((None, 16, 512), lambda b, c: (b, 0, _conv_col(c))),
                  pl.BlockSpec((CONV_K, 512), lambda b, c: (0, c)),
                  pl.BlockSpec((1, 512), lambda b, c: (0, c))],
        out_specs=pl.BlockSpec((None, CHUNK, 512), lambda b, c: (b, 0, c)),
        scratch_shapes=[pltpu.VMEM((48, 512), F32)],
        compiler_params=_cp(("parallel", "parallel")),
        name="conv_meta",
    )(proj_meta, proj3, cw, cb)


def _dt_kernel(raw_ref, bias_ref, alog_ref, dt_ref, cc_ref, cr_ref, *, cpb, valid_from):
    row = lax.broadcasted_iota(jnp.int32, (CHUNK, 128), 0)
    col = lax.broadcasted_iota(jnp.int32, (CHUNK, 128), 1)
    tri = (col <= row).astype(F32)
    triu = (col >= row).astype(F32)
    a = -jnp.exp(alog_ref[...]) * LOG2E
    for c in range(cpb):
        rs = slice(c * CHUNK, (c + 1) * CHUNK)
        x = raw_ref[rs, :] + bias_ref[...]
        dt = jnp.maximum(x, 0.0) + jnp.log(1.0 + jnp.exp(-jnp.abs(x)))
        dt = jnp.where(col < 2 * SSD_H, dt, 0.0)
        if valid_from:
            dt = jnp.where(row >= valid_from, dt, 0.0)
        da = dt * a
        cf = jnp.dot(tri, da, precision=lax.Precision.HIGHEST, preferred_element_type=F32)
        cb = jnp.dot(triu, da, precision=lax.Precision.HIGHEST, preferred_element_type=F32)
        cum = jnp.where(col < SSD_H, cf, cb)
        dt_ref[rs, :] = dt
        cc_ref[rs, :] = cum
        cr_ref[c] = cum.T


def _dtprep(raw, bias, alog, cpb, valid_from):
    rows = raw.shape[0]
    nch = rows // CHUNK
    blk = cpb * CHUNK
    o = jax.ShapeDtypeStruct((rows, 128), F32)
    return pl.pallas_call(
        functools.partial(_dt_kernel, cpb=cpb, valid_from=valid_from),
        out_shape=(o, o, jax.ShapeDtypeStruct((nch, 128, CHUNK), F32)),
        grid=(nch // cpb,),
        in_specs=[pl.BlockSpec((blk, 128), lambda i: (i, 0)),
                  pl.BlockSpec((1, 128), lambda i: (0, 0)),
                  pl.BlockSpec((1, 128), lambda i: (0, 0))],
        out_specs=(pl.BlockSpec((blk, 128), lambda i: (i, 0)),
                   pl.BlockSpec((blk, 128), lambda i: (i, 0)),
                   pl.BlockSpec((cpb, 128, CHUNK), lambda i: (i, 0, 0))),
        compiler_params=_cp(("parallel",)),
        name="dtprep",
    )(raw, bias, alog)


def _ssd_chunk(state, xbc, dtc, cc, cr, g, fwd, want_y):
    row = lax.broadcasted_iota(jnp.int32, (CHUNK, CHUNK), 0)
    col = lax.broadcasted_iota(jnp.int32, (CHUNK, CHUNK), 1)
    keep = (col <= row) if fwd else (col >= row)
    lo = col < SSD_P
    hb = (0 if fwd else SSD_H) + g * SSD_J
    edge = CHUNK - 1 if fwd else 0

    xs = xbc[:, g * 512:(g + 1) * 512].astype(F32)
    bm = xbc[:, SSD_W + g * D_STATE:SSD_W + (g + 1) * D_STATE]
    cm = xbc[:, SSD_W + GN + g * D_STATE:SSD_W + GN + (g + 1) * D_STATE]
    bt = bm.astype(F32).T.astype(BF16)
    if want_y:
        cbm = lax.dot_general(cm, bm, (((1,), (1,)), ((), ())), preferred_element_type=F32)
        yoff = jnp.dot(cm, state.astype(BF16), preferred_element_type=F32)

    def bcl(a, j):
        return jnp.broadcast_to(a[:, hb + j:hb + j + 1], (CHUNK, CHUNK))

    ys, xws, decs = [], [], []
    for kk in range(SSD_J // 2):
        j0, j1 = 2 * kk, 2 * kk + 1
        cc0, cc1 = bcl(cc, j0), bcl(cc, j1)
        cc_pair = jnp.where(lo, cc0, cc1)
        dt_pair = jnp.take_along_axis(dtc, jnp.where(lo, hb + j0, hb + j1), axis=1)
        xdt = xs[:, kk * 128:(kk + 1) * 128] * dt_pair
        tot = cc_pair[edge:edge + 1, :]
        xws.append((xdt * jnp.exp2(tot - cc_pair)).astype(BF16))
        decs.append(jnp.exp2(tot))
        if want_y:
            xdt_b = xdt.astype(BF16)

            def ydiag(ccj, j):
                crj = jnp.broadcast_to(cr[hb + j:hb + j + 1, :], (CHUNK, CHUNK))
                decay = jnp.exp2(jnp.where(keep, ccj - crj, NEG_INF))
                return jnp.dot((cbm * decay).astype(BF16), xdt_b, preferred_element_type=F32)

            yd = jnp.where(lo, ydiag(cc0, j0), ydiag(cc1, j1))
            ys.append(yd + yoff[:, kk * 128:(kk + 1) * 128] * jnp.exp2(cc_pair))
    xw = jnp.concatenate(xws, axis=1)
    dec = jnp.concatenate(decs, axis=1)
    new_state = state * dec + jnp.dot(bt, xw, preferred_element_type=F32)
    return new_state, (jnp.concatenate(ys, axis=1) if want_y else None)


def _ssd_scan_kernel(xf_ref, dtf_ref, ccf_ref, crf_ref, xb_ref, dtb_ref, ccb_ref, crb_ref,
                     xm_ref, dtm_ref, ccm_ref, crm_ref, dsk_ref, yf_ref, yb_ref, stf_ref, stb_ref, *, ncs):
    s = pl.program_id(1)

    @pl.when(s == 0)
    def _():
        stb_ref[...] = jnp.zeros_like(stb_ref)
        for g in range(SSD_G):
            st, _ = _ssd_chunk(jnp.zeros((D_STATE, 512), F32), xm_ref[...], dtm_ref[...], ccm_ref[...],
                               crm_ref[0], g, True, False)
            stf_ref[g] = st

    def body(i, carry):
        cb = ncs - 1 - i
        rf = pl.ds(pl.multiple_of(i * CHUNK, CHUNK), CHUNK)
        rb = pl.ds(pl.multiple_of(cb * CHUNK, CHUNK), CHUNK)
        for g in range(SSD_G):
            gs = slice(g * 512, (g + 1) * 512)
            xf = xf_ref[rf, :]
            st, y = _ssd_chunk(stf_ref[g], xf, dtf_ref[rf, :], ccf_ref[rf, :], crf_ref[i], g, True, True)
            stf_ref[g] = st
            yf_ref[rf, gs] = (y + dsk_ref[:, gs] * xf[:, gs].astype(F32)).astype(BF16)
            st, y = _ssd_chunk(stb_ref[g], xb_ref[rb, :], dtb_ref[rb, :], ccb_ref[rb, :], crb_ref[cb],
                               g, False, True)
            stb_ref[g] = st
            yb_ref[rb, gs] = y.astype(BF16)
        return carry

    lax.fori_loop(0, ncs, body, 0)


def _ssd(xbc, dt, cc, cr, xbc_m, dt_m, cc_m, cr_m, dsk, ts):
    bsz, seq, _ = xbc.shape
    ncs = ts // CHUNK
    nseg = seq // ts
    dt3 = dt.reshape(bsz, seq, 128)
    cc3 = cc.reshape(bsz, seq, 128)
    cr4 = cr.reshape(bsz, seq // CHUNK, 128, CHUNK)
    st = pltpu.VMEM((SSD_G, D_STATE, 512), F32)
    y_shape = jax.ShapeDtypeStruct((bsz, seq, SSD_W), BF16)

    def seg_specs(rev):
        ix = (lambda s: nseg - 1 - s) if rev else (lambda s: s)
        return [pl.BlockSpec((None, ts, CONV_DIM), lambda b, s: (b, ix(s), 0)),
                pl.BlockSpec((None, ts, 128), lambda b, s: (b, ix(s), 0)),
                pl.BlockSpec((None, ts, 128), lambda b, s: (b, ix(s), 0)),
                pl.BlockSpec((None, ncs, 128, CHUNK), lambda b, s: (b, ix(s), 0, 0))]

    return pl.pallas_call(
        functools.partial(_ssd_scan_kernel, ncs=ncs),
        out_shape=(y_shape, y_shape),
        grid=(bsz, nseg),
        in_specs=seg_specs(False) + seg_specs(True) + [
            pl.BlockSpec((None, CHUNK, CONV_DIM), lambda b, s: (b, 0, 0)),
            pl.BlockSpec((CHUNK, 128), lambda b, s: (0, 0)),
            pl.BlockSpec((CHUNK, 128), lambda b, s: (0, 0)),
            pl.BlockSpec((1, 128, CHUNK), lambda b, s: (0, 0, 0)),
            pl.BlockSpec((1, SSD_W), lambda b, s: (0, 0))],
        out_specs=(pl.BlockSpec((None, ts, SSD_W), lambda b, s: (b, s, 0)),
                   pl.BlockSpec((None, ts, SSD_W), lambda b, s: (b, nseg - 1 - s, 0))),
        scratch_shapes=[st, st],
        compiler_params=_cp(("parallel", "arbitrary")),
        name="ssd_scan",
    )(xbc, dt3, cc3, cr4, xbc, dt3, cc3, cr4, xbc_m, dt_m, cc_m, cr_m, dsk)


def _out_kernel(o_ref, g_ref, yf_ref, yb_ref, z_ref, mg_ref, x_ref, snw_ref, wa_ref, ws_ref, wo_ref, fnw_ref,
                out_ref):
    g = g_ref[...].astype(F32)
    ag = (o_ref[...].astype(F32) * (g * _sigmoid(g))).astype(BF16)
    ya = jnp.dot(ag, wa_ref[...], preferred_element_type=F32)
    gw = SSD_W // SSD_G
    yn = []
    for grp in range(SSD_G):
        gs = slice(grp * gw, (grp + 1) * gw)
        z = z_ref[:, gs].astype(F32)
        y = yf_ref[:, gs].astype(F32) + yb_ref[:, gs].astype(F32)
        yn.append(_rms(y * (z * _sigmoid(z)), snw_ref[:, gs]).astype(BF16))
    ys = jnp.dot(jnp.concatenate(yn, axis=-1), ws_ref[...], preferred_element_type=F32)
    gate = _sigmoid(mg_ref[...].astype(F32))
    mixed = (gate[:, :D_MODEL] * ya + gate[:, D_MODEL:] * ys).astype(BF16)
    h = x_ref[...] + jnp.dot(mixed, wo_ref[...], preferred_element_type=F32)
    out_ref[...] = _rms(h, fnw_ref[...])


def _outproj(o2, proj, yf2, yb2, x2, snw, wa, ws, wo, fnw, tm):
    rows = x2.shape[0]
    full = lambda shape: pl.BlockSpec(shape, lambda i: (0, 0))
    return pl.pallas_call(
        _out_kernel,
        out_shape=jax.ShapeDtypeStruct((rows, D_MODEL), F32),
        grid=(rows // tm,),
        in_specs=[pl.BlockSpec((tm, ATTN_W), lambda i: (i, 0)),
                  pl.BlockSpec((tm, ATTN_W), lambda i: (i, C_G // ATTN_W)),
                  pl.BlockSpec((tm, SSD_W), lambda i: (i, 0)),
                  pl.BlockSpec((tm, SSD_W), lambda i: (i, 0)),
                  pl.BlockSpec((tm, SSD_W), lambda i: (i, C_Z // SSD_W)),
                  pl.BlockSpec((tm, 2 * D_MODEL), lambda i: (i, C_MG // (2 * D_MODEL))),
                  pl.BlockSpec((tm, D_MODEL), lambda i: (i, 0)),
                  full((1, SSD_W)), full((ATTN_W, D_MODEL)), full((SSD_W, D_MODEL)), full((D_MODEL, D_MODEL)),
                  full((1, D_MODEL))],
        out_specs=pl.BlockSpec((tm, D_MODEL), lambda i: (i, 0)),
        compiler_params=_cp(("parallel",)),
        name="outproj",
    )(o2, proj, yf2, yb2, proj, proj, x2, snw, wa, ws, wo, fnw)


def _rope_tables(pos):
    inv_freq = ROPE_THETA ** (-jnp.arange(0, ROPE, 2, dtype=F32) / ROPE)
    ang = pos.astype(F32)[:, None] * inv_freq[None, :]
    cos, sin = jnp.cos(ang), jnp.sin(ang)
    n = pos.shape[0]
    one = jnp.ones((n, NOPE), F32)
    z16 = jnp.zeros((n, ROPE // 2), F32)
    z32 = jnp.zeros((n, HP - QK_DIM), F32)
    z64 = jnp.zeros((n, NOPE), F32)
    cos_t = jnp.concatenate([one, cos, cos, z32 + 1.0], axis=1)
    sin_p = jnp.concatenate([z64, z16, sin, z32], axis=1)
    sin_m = jnp.concatenate([z64, -sin, z16, z32], axis=1)
    return cos_t, sin_p, sin_m


def _meta_blockdiag(t):
    tv = t[META_PAD:].reshape(N_META, N_HEADS // 2, 2, HP)
    z = jnp.zeros_like(tv[:, :, 0])
    top = jnp.concatenate([tv[:, :, 0], z], axis=-1)
    bot = jnp.concatenate([z, tv[:, :, 1]], axis=-1)
    pad = jnp.zeros((CHUNK - 2 * N_META, N_HEADS // 2, 2 * HP), t.dtype)
    return jnp.swapaxes(jnp.concatenate([top, bot, pad], axis=0), 0, 1)


def _pick(n, prefs):
    for t in prefs:
        if n % t == 0:
            return t
    raise ValueError(f"no tile for {n}")


def kernel(x, meta_tokens, norm_w, w_in, q_norm_w, w_uq, kv_norm_w, w_ukv, w_attn_proj, conv_w, conv_b,
           dt_bias, a_log, d_skip, ssd_norm_w, w_ssd_proj, w_out, final_norm_w):
    bsz, seq, d = x.shape
    assert d == D_MODEL and seq % 128 == 0 and norm_w.shape[0] == 1
    rows = bsz * seq
    zc = lambda n: jnp.zeros((D_MODEL, n), F32)
    w = w_in[0]
    w_p = jnp.concatenate([
        w[:, O_KV:O_KV + KV_RANK], zc(NOPE), w[:, O_KR:O_KR + ROPE], zc(HP - QK_DIM),
        w[:, O_Q:O_Q + Q_RANK], w[:, O_DT:O_DT + 2 * SSD_H], zc(128 - 2 * SSD_H), zc(128),
        w[:, O_G:O_G + ATTN_W], w[:, O_Z:O_Z + SSD_W], w[:, O_XBC:O_XBC + SSD_W],
        w[:, O_MG:O_MG + 2 * D_MODEL], w[:, O_XBC + SSD_W:O_XBC + CONV_DIM]], axis=1).astype(BF16)
    assert w_p.shape[1] == NP
    pad_h = lambda t: jnp.pad(t, ((0, 0), (0, 0), (0, HP - t.shape[-1]))).reshape(t.shape[0], N_HEADS * HP)
    wq = pad_h(w_uq[0].reshape(Q_RANK, N_HEADS, QK_DIM)).astype(BF16)
    assert NOPE + V_DIM == HP
    wkv = w_ukv[0].astype(BF16)
    row = lambda t: t.reshape(1, -1)
    pad128 = lambda t: jnp.pad(t.reshape(1, -1), ((0, 0), (0, 128 - t.size)))

    tm = _pick(rows, (2048, 1024, 512, 256, 128))
    x2 = x.reshape(rows, D_MODEL)
    proj, dt_raw = _inproj(x2, row(norm_w[0]), w_p, tm)
    xm = jnp.concatenate([jnp.zeros((META_PAD, D_MODEL), F32), meta_tokens.astype(F32)], axis=0)
    proj_m, dt_raw_m = _inproj(xm, row(norm_w[0]), w_p, CHUNK)

    tabs = _rope_tables(N_META + jnp.arange(seq))
    tabs_m = _rope_tables(jnp.maximum(jnp.arange(CHUNK) - META_PAD, 0))
    tq2 = _pick(seq, (512, 256, 128))
    q, k, v = _qkv(proj, tabs, row(q_norm_w[0]), row(kv_norm_w[0]), wq, wkv, tq2, seq // tq2)
    _, km, vm = _qkv(proj_m, tabs_m, row(q_norm_w[0]), row(kv_norm_w[0]), wq, wkv, CHUNK, 1)

    r3 = lambda t: t.reshape(bsz, seq, t.shape[-1])
    tq = _pick(seq, (2048, 1024, 512, 256, 128))
    tk = _pick(seq, (512, 256, 128))
    o = _flash(r3(q), r3(k), r3(v), _meta_blockdiag(km), _meta_blockdiag(vm), tq, tk)

    proj3 = r3(proj)
    cw, cb = conv_w[0], row(conv_b[0])
    xbc = _conv(proj3, proj_m, cw, cb, _pick(seq, (2048, 1024, 512, 256, 128)))
    xbc_m = _conv_meta(proj3, proj_m, cw, cb)
    bias, alog = pad128(dt_bias[0]), pad128(a_log[0])
    dt, cc, cr = _dtprep(dt_raw, bias, alog, _pick(rows // CHUNK, (8, 4, 2, 1)), 0)
    dt_m, cc_m, cr_m = _dtprep(dt_raw_m, bias, alog, 1, META_PAD)
    dsk = row(jnp.repeat(d_skip[0], SSD_P))
    yf, yb = _ssd(xbc, dt, cc, cr, xbc_m, dt_m, cc_m, cr_m, dsk, _pick(seq, (512, 256, 128)))

    out = _outproj(o.reshape(rows, ATTN_W), proj, yf.reshape(rows, SSD_W), yb.reshape(rows, SSD_W), x2,
                   row(ssd_norm_w[0]),
                   w_attn_proj[0].astype(BF16), w_ssd_proj[0].astype(BF16), w_out[0].astype(BF16),
                   row(final_norm_w), _pick(rows, (512, 256, 128)))
    return out.reshape(bsz, seq, D_MODEL)
```

```python
import functools
import math

import jax
import jax.numpy as jnp
from jax import lax
from jax.experimental import pallas as pl
from jax.experimental.pallas import tpu as pltpu

F32 = jnp.float32
BF16 = jnp.bfloat16

D_MODEL = 1024
N_META = 16
N_HEADS = 16
NOPE = 64
ROPE = 32
QK_DIM = NOPE + ROPE
V_DIM = 64
Q_RANK = 384
KV_RANK = 256
ROPE_THETA = 10000.0
ATTN_W = N_HEADS * V_DIM
SSD_W = 2048
SSD_P = 64
SSD_H = SSD_W // SSD_P
SSD_G = 4
SSD_J = SSD_H // SSD_G
D_STATE = 128
CONV_K = 5
GN = SSD_G * D_STATE
CONV_DIM = SSD_W + 2 * GN
CHUNK = 128
EPS = 1e-6
HP = 128
META_PAD = CHUNK - N_META

C_KV, C_KR, C_Q, C_DT = 0, 256, 384, 768
C_G, C_Z, C_XS, C_MG, C_B, C_C = 1024, 2048, 4096, 6144, 8192, 8704
NP = 9216
O_Q, O_KV, O_KR, O_G, O_Z, O_XBC, O_DT, O_MG = 0, 384, 640, 672, 1696, 3744, 6816, 6880

VMEM_LIMIT = 56 * 1024 * 1024
NEG_INF = float("-inf")
LOG2E = math.log2(math.e)


def _cp(sem):
    return pltpu.CompilerParams(dimension_semantics=sem, vmem_limit_bytes=VMEM_LIMIT)


def _sigmoid(x):
    return 1.0 / (1.0 + jnp.exp(-x))


def _rms(x, w):
    return x * lax.rsqrt(jnp.mean(x * x, axis=-1, keepdims=True) + EPS) * w


def _inproj_kernel(x_ref, nw_ref, w_ref, o_ref, dt_ref, u_ref):
    j = pl.program_id(1)

    @pl.when(j == 0)
    def _():
        u_ref[...] = _rms(x_ref[...], nw_ref[...]).astype(BF16)

    acc = jnp.dot(u_ref[...], w_ref[...], preferred_element_type=F32)
    o_ref[...] = acc.astype(BF16)

    @pl.when(j == 0)
    def _():
        dt_ref[...] = acc[:, C_DT:C_DT + 128]


def _inproj(x2, nw, w, tm, tn=1024):
    rows = x2.shape[0]
    return pl.pallas_call(
        _inproj_kernel,
        out_shape=(jax.ShapeDtypeStruct((rows, NP), BF16),
                   jax.ShapeDtypeStruct((rows, 128), F32)),
        grid=(rows // tm, NP // tn),
        in_specs=[pl.BlockSpec((tm, D_MODEL), lambda i, j: (i, 0)),
                  pl.BlockSpec((1, D_MODEL), lambda i, j: (0, 0)),
                  pl.BlockSpec((D_MODEL, tn), lambda i, j: (0, j))],
        out_specs=(pl.BlockSpec((tm, tn), lambda i, j: (i, j)),
                   pl.BlockSpec((tm, 128), lambda i, j: (i, 0))),
        scratch_shapes=[pltpu.VMEM((tm, D_MODEL), BF16)],
        compiler_params=_cp(("parallel", "arbitrary")),
        name="inproj",
    )(x2, nw, w)


def _qkv_kernel(hd_ref, cos_ref, sp_ref, sm_ref, qnw_ref, kvnw_ref, wq_ref, wkv_ref,
                q_ref, k_ref, v_ref, *, scale):
    cos_t, sin_p, sin_m = cos_ref[...], sp_ref[...], sm_ref[...]

    def rope(t):
        return t * cos_t + pltpu.roll(t, 16, 1) * sin_p + pltpu.roll(t, 112, 1) * sin_m

    kv_lat = hd_ref[:, C_KV:C_KV + KV_RANK].astype(F32)
    kvn = _rms(kv_lat, kvnw_ref[...]).astype(BF16)
    kv = jnp.dot(kvn, wkv_ref[...], preferred_element_type=F32)
    kpe = rope(hd_ref[:, C_KR:C_KR + HP].astype(F32))
    lane = lax.broadcasted_iota(jnp.int32, kpe.shape, 1)
    for h in range(N_HEADS):
        sl = slice(h * HP, (h + 1) * HP)
        k_ref[:, sl] = jnp.where(lane < NOPE, kv[:, sl], kpe).astype(BF16)
        v_ref[:, sl] = jnp.where(lane < NOPE, 1.0, kv[:, sl]).astype(BF16)

    q_lat = hd_ref[:, C_Q:C_Q + Q_RANK].astype(F32)
    qn = _rms(q_lat, qnw_ref[...]).astype(BF16)
    q = jnp.dot(qn, wq_ref[...], preferred_element_type=F32)
    for h in range(N_HEADS):
        sl = slice(h * HP, (h + 1) * HP)
        q_ref[:, sl] = (rope(q[:, sl]) * scale).astype(BF16)


def _qkv(proj, tabs, qnw, kvnw, wq, wkv, tm, tiles_per_seq):
    rows = proj.shape[0]
    w = N_HEADS * HP
    tab_spec = pl.BlockSpec((tm, HP), lambda i: (i % tiles_per_seq, 0))
    out = jax.ShapeDtypeStruct((rows, w), BF16)
    return pl.pallas_call(
        functools.partial(_qkv_kernel, scale=LOG2E * QK_DIM ** -0.5),
        out_shape=(out, out, out),
        grid=(rows // tm,),
        in_specs=[pl.BlockSpec((tm, 1024), lambda i: (i, 0)),
                  tab_spec, tab_spec, tab_spec,
                  pl.BlockSpec((1, Q_RANK), lambda i: (0, 0)),
                  pl.BlockSpec((1, KV_RANK), lambda i: (0, 0)),
                  pl.BlockSpec((Q_RANK, w), lambda i: (0, 0)),
                  pl.BlockSpec((KV_RANK, w), lambda i: (0, 0))],
        out_specs=(pl.BlockSpec((tm, w), lambda i: (i, 0)),) * 3,
        compiler_params=_cp(("parallel",)),
        name="qkv",
    )(proj, *tabs, qnw, kvnw, wq, wkv)


def _flash_kernel(q_ref, k_ref, v_ref, km_ref, vm_ref, o_ref, m_ref, acc_ref, *, tk, nk):
    nt = (((1,), (1,)), ((), ()))
    heads = [slice(hh * HP, (hh + 1) * HP) for hh in range(2)]
    s0 = lax.dot_general(q_ref[...], km_ref[...], nt, preferred_element_type=F32)
    col = lax.broadcasted_iota(jnp.int32, s0.shape, 1)
    s0 = jnp.where(col < 2 * N_META, s0, NEG_INF)
    m0 = jnp.max(jnp.where(col < N_META, s0, NEG_INF), axis=-1, keepdims=True)
    m1 = jnp.max(jnp.where(col >= N_META, s0, NEG_INF), axis=-1, keepdims=True)
    p0 = jnp.exp2(s0 - jnp.where(col < N_META, m0, m1)).astype(BF16)
    acc0 = jnp.dot(p0, vm_ref[...], preferred_element_type=F32)
    for hh, (sl, m) in enumerate(zip(heads, (m0, m1))):
        m_ref[hh] = jnp.broadcast_to(m, s0.shape)
        acc_ref[hh] = acc0[:, sl]

    def body(j, carry):
        off = pl.multiple_of(j * tk, tk)
        ss = [lax.dot_general(q_ref[:, sl], k_ref[pl.ds(off, tk), sl], nt, preferred_element_type=F32)
              for sl in heads]
        for hh, sl in enumerate(heads):
            blocks = [ss[hh][:, c * HP:(c + 1) * HP] for c in range(tk // HP)]
            m = m_ref[hh]
            m_new = jnp.maximum(m, jnp.max(functools.reduce(jnp.maximum, blocks), axis=-1, keepdims=True))
            p = jnp.concatenate([jnp.exp2(blk - m_new) for blk in blocks], axis=-1).astype(BF16)
            pv = jnp.dot(p, v_ref[pl.ds(off, tk), sl], preferred_element_type=F32)
            acc_ref[hh] = jnp.exp2(m - m_new) * acc_ref[hh] + pv
            m_ref[hh] = m_new
        return carry

    lax.fori_loop(0, nk, body, 0, unroll=4)
    a0, a1 = acc_ref[0], acc_ref[1]
    lane = lax.broadcasted_iota(jnp.int32, a0.shape, 1)
    o_ref[...] = jnp.where(lane < V_DIM, pltpu.roll(a0, V_DIM, 1) / a0, a1 / pltpu.roll(a1, V_DIM, 1)).astype(BF16)


def _flash(q, k, v, km, vm, tq, tk):
    bsz, seq, _ = q.shape
    return pl.pallas_call(
        functools.partial(_flash_kernel, tk=tk, nk=seq // tk),
        out_shape=jax.ShapeDtypeStruct((bsz, seq, ATTN_W), BF16),
        grid=(bsz, N_HEADS // 2, seq // tq),
        in_specs=[pl.BlockSpec((None, tq, 2 * HP), lambda b, h, i: (b, i, h)),
                  pl.BlockSpec((None, seq, 2 * HP), lambda b, h, i: (b, 0, h)),
                  pl.BlockSpec((None, seq, 2 * HP), lambda b, h, i: (b, 0, h)),
                  pl.BlockSpec((None, CHUNK, 2 * HP), lambda b, h, i: (h, 0, 0)),
                  pl.BlockSpec((None, CHUNK, 2 * HP), lambda b, h, i: (h, 0, 0))],
        out_specs=pl.BlockSpec((None, tq, 2 * V_DIM), lambda b, h, i: (b, i, h)),
        scratch_shapes=[pltpu.VMEM((2, tq, HP), F32), pltpu.VMEM((2, tq, HP), F32)],
        compiler_params=_cp(("parallel", "parallel", "arbitrary")),
        name="flash",
    )(q, k, v, km, vm)


def _conv_taps(ext_ref, w_ref, b_ref, rows):
    ext = ext_ref[...]
    n = ext.shape[0]
    acc = b_ref[...] + w_ref[CONV_K // 2:CONV_K // 2 + 1, :] * ext[16:16 + rows]
    for t in range(CONV_K):
        if t != CONV_K // 2:
            shifted = pltpu.roll(ext, (CONV_K // 2 - t) % n, 0)
            acc = acc + w_ref[t:t + 1, :] * shifted[16:16 + rows]
    return acc * _sigmoid(acc)


def _conv_kernel(main_ref, prev_ref, next_ref, meta_ref, w_ref, b_ref, o_ref, ext_ref, *, tr, nr):
    r = pl.program_id(2)
    prev = jnp.where(r == 0, meta_ref[CHUNK - 16:CHUNK, :], prev_ref[...])
    nxt = jnp.where(r == nr - 1, jnp.zeros_like(next_ref[...]), next_ref[...])
    ext_ref[0:16, :] = prev.astype(F32)
    ext_ref[16:16 + tr, :] = main_ref[...].astype(F32)
    ext_ref[16 + tr:32 + tr, :] = nxt.astype(F32)
    o_ref[...] = _conv_taps(ext_ref, w_ref, b_ref, tr).astype(BF16)


def _conv_col(c):
    return jnp.where(c < 4, C_XS // 512 + c, C_B // 512 + (c - 4))


def _conv(proj3, proj_meta, cw, cb, tr):
    bsz, seq, _ = proj3.shape
    nr = seq // tr
    hb = tr // 16
    return pl.pallas_call(
        functools.partial(_conv_kernel, tr=tr, nr=nr),
        out_shape=jax.ShapeDtypeStruct((bsz, seq, CONV_DIM), BF16),
        grid=(bsz, CONV_DIM // 512, nr),
        in_specs=[pl.BlockSpec((None, tr, 512), lambda b, c, r: (b, r, _conv_col(c))),
                  pl.BlockSpec((None, 16, 512),
                               lambda b, c, r: (b, jnp.maximum(r * hb - 1, 0), _conv_col(c))),
                  pl.BlockSpec((None, 16, 512),
                               lambda b, c, r: (b, jnp.minimum((r + 1) * hb, seq // 16 - 1), _conv_col(c))),
                  pl.BlockSpec((CHUNK, 512), lambda b, c, r: (0, _conv_col(c))),
                  pl.BlockSpec((CONV_K, 512), lambda b, c, r: (0, c)),
                  pl.BlockSpec((1, 512), lambda b, c, r: (0, c))],
        out_specs=pl.BlockSpec((None, tr, 512), lambda b, c, r: (b, r, c)),
        scratch_shapes=[pltpu.VMEM((tr + 32, 512), F32)],
        compiler_params=_cp(("parallel", "parallel", "arbitrary")),
        name="conv",
    )(proj3, proj3, proj3, proj_meta, cw, cb)


def _conv_meta_kernel(meta_ref, main_ref, w_ref, b_ref, o_ref, ext_ref):
    ext_ref[0:32, :] = meta_ref[CHUNK - 32:CHUNK, :].astype(F32)
    ext_ref[32:48, :] = main_ref[...].astype(F32)
    o_ref[0:META_PAD, :] = jnp.zeros((META_PAD, 512), BF16)
    o_ref[META_PAD:CHUNK, :] = _conv_taps(ext_ref, w_ref, b_ref, N_META).astype(BF16)


def _conv_meta(proj3, proj_meta, cw, cb):
    bsz = proj3.shape[0]
    return pl.pallas_call(
        _conv_meta_kernel,
        out_shape=jax.ShapeDtypeStruct((bsz, CHUNK, CONV_DIM), BF16),
        grid=(bsz, CONV_DIM // 512),
        in_specs=[pl.BlockSpec((CHUNK, 512), lambda b, c: (0, _conv_col(c))),
                  pl.BlockSpec((None, 16, 512), lambda b, c: (b, 0, _conv_col(c))),
                  pl.BlockSpec((CONV_K, 512), lambda b, c: (0, c)),
                  pl.BlockSpec((1, 512), lambda b, c: (0, c))],
        out_specs=pl.BlockSpec((None, CHUNK, 512), lambda b, c: (b, 0, c)),
        scratch_shapes=[pltpu.VMEM((48, 512), F32)],
        compiler_params=_cp(("parallel", "parallel")),
        name="conv_meta",
    )(proj_meta, proj3, cw, cb)


def _dt_kernel(raw_ref, bias_ref, alog_ref, dt_ref, cc_ref, cr_ref, *, cpb, valid_from):
    row = lax.broadcasted_iota(jnp.int32, (CHUNK, 128), 0)
    col = lax.broadcasted_iota(jnp.int32, (CHUNK, 128), 1)
    tri = (col <= row).astype(F32)
    triu = (col >= row).astype(F32)
    a = -jnp.exp(alog_ref[...]) * LOG2E
    for c in range(cpb):
        rs = slice(c * CHUNK, (c + 1) * CHUNK)
        x = raw_ref[rs, :] + bias_ref[...]
        dt = jnp.maximum(x, 0.0) + jnp.log(1.0 + jnp.exp(-jnp.abs(x)))
        dt = jnp.where(col < 2 * SSD_H, dt, 0.0)
        if valid_from:
            dt = jnp.where(row >= valid_from, dt, 0.0)
        da = dt * a
        cf = jnp.dot(tri, da, precision=lax.Precision.HIGHEST, preferred_element_type=F32)
        cb = jnp.dot(triu, da, precision=lax.Precision.HIGHEST, preferred_element_type=F32)
        cum = jnp.where(col < SSD_H, cf, cb)
        dt_ref[rs, :] = dt
        cc_ref[rs, :] = cum
        cr_ref[c] = cum.T


def _dtprep(raw, bias, alog, cpb, valid_from):
    rows = raw.shape[0]
    nch = rows // CHUNK
    blk = cpb * CHUNK
    o = jax.ShapeDtypeStruct((rows, 128), F32)
    return pl.pallas_call(
        functools.partial(_dt_kernel, cpb=cpb, valid_from=valid_from),
        out_shape=(o, o, jax.ShapeDtypeStruct((nch, 128, CHUNK), F32)),
        grid=(nch // cpb,),
        in_specs=[pl.BlockSpec((blk, 128), lambda i: (i, 0)),
                  pl.BlockSpec((1, 128), lambda i: (0, 0)),
                  pl.BlockSpec((1, 128), lambda i: (0, 0))],
        out_specs=(pl.BlockSpec((blk, 128), lambda i: (i, 0)),
                   pl.BlockSpec((blk, 128), lambda i: (i, 0)),
                   pl.BlockSpec((cpb, 128, CHUNK), lambda i: (i, 0, 0))),
        compiler_params=_cp(("parallel",)),
        name="dtprep",
    )(raw, bias, alog)


def _ssd_chunk(state, xbc, dtc, cc, cr, g, fwd, want_y):
    row = lax.broadcasted_iota(jnp.int32, (CHUNK, CHUNK), 0)
    col = lax.broadcasted_iota(jnp.int32, (CHUNK, CHUNK), 1)
    keep = (col <= row) if fwd else (col >= row)
    lo = col < SSD_P
    hb = (0 if fwd else SSD_H) + g * SSD_J
    edge = CHUNK - 1 if fwd else 0

    xs = xbc[:, g * 512:(g + 1) * 512].astype(F32)
    bm = xbc[:, SSD_W + g * D_STATE:SSD_W + (g + 1) * D_STATE]
    cm = xbc[:, SSD_W + GN + g * D_STATE:SSD_W + GN + (g + 1) * D_STATE]
    bt = bm.astype(F32).T.astype(BF16)
    if want_y:
        cbm = lax.dot_general(cm, bm, (((1,), (1,)), ((), ())), preferred_element_type=F32)
        yoff = jnp.dot(cm, state.astype(BF16), preferred_element_type=F32)

    def bcl(a, j):
        return jnp.broadcast_to(a[:, hb + j:hb + j + 1], (CHUNK, CHUNK))

    ys, xws, decs = [], [], []
    for kk in range(SSD_J // 2):
        j0, j1 = 2 * kk, 2 * kk + 1
        cc0, cc1 = bcl(cc, j0), bcl(cc, j1)
        cc_pair = jnp.where(lo, cc0, cc1)
        dt_pair = jnp.take_along_axis(dtc, jnp.where(lo, hb + j0, hb + j1), axis=1)
        xdt = xs[:, kk * 128:(kk + 1) * 128] * dt_pair
        tot = cc_pair[edge:edge + 1, :]
        xws.append((xdt * jnp.exp2(tot - cc_pair)).astype(BF16))
        decs.append(jnp.exp2(tot))
        if want_y:
            xdt_b = xdt.astype(BF16)

            def ydiag(ccj, j):
                crj = jnp.broadcast_to(cr[hb + j:hb + j + 1, :], (CHUNK, CHUNK))
                decay = jnp.exp2(jnp.where(keep, ccj - crj, NEG_INF))
                return jnp.dot((cbm * decay).astype(BF16), xdt_b, preferred_element_type=F32)

            yd = jnp.where(lo, ydiag(cc0, j0), ydiag(cc1, j1))
            ys.append(yd + yoff[:, kk * 128:(kk + 1) * 128] * jnp.exp2(cc_pair))
    xw = jnp.concatenate(xws, axis=1)
    dec = jnp.concatenate(decs, axis=1)
    new_state = state * dec + jnp.dot(bt, xw, preferred_element_type=F32)
    return new_state, (jnp.concatenate(ys, axis=1) if want_y else None)


def _ssd_scan_kernel(xf_ref, dtf_ref, ccf_ref, crf_ref, xb_ref, dtb_ref, ccb_ref, crb_ref,
                     xm_ref, dtm_ref, ccm_ref, crm_ref, dsk_ref, yf_ref, yb_ref, stf_ref, stb_ref, *, ncs):
    s = pl.program_id(1)

    @pl.when(s == 0)
    def _():
        stb_ref[...] = jnp.zeros_like(stb_ref)
        for g in range(SSD_G):
            st, _ = _ssd_chunk(jnp.zeros((D_STATE, 512), F32), xm_ref[...], dtm_ref[...], ccm_ref[...],
                               crm_ref[0], g, True, False)
            stf_ref[g] = st

    def body(i, carry):
        cb = ncs - 1 - i
        rf = pl.ds(pl.multiple_of(i * CHUNK, CHUNK), CHUNK)
        rb = pl.ds(pl.multiple_of(cb * CHUNK, CHUNK), CHUNK)
        for g in range(SSD_G):
            gs = slice(g * 512, (g + 1) * 512)
            xf = xf_ref[rf, :]
            st, y = _ssd_chunk(stf_ref[g], xf, dtf_ref[rf, :], ccf_ref[rf, :], crf_ref[i], g, True, True)
            stf_ref[g] = st
            yf_ref[rf, gs] = (y + dsk_ref[:, gs] * xf[:, gs].astype(F32)).astype(BF16)
            st, y = _ssd_chunk(stb_ref[g], xb_ref[rb, :], dtb_ref[rb, :], ccb_ref[rb, :], crb_ref[cb],
                               g, False, True)
            stb_ref[g] = st
            yb_ref[rb, gs] = y.astype(BF16)
        return carry

    lax.fori_loop(0, ncs, body, 0)


def _ssd(xbc, dt, cc, cr, xbc_m, dt_m, cc_m, cr_m, dsk, ts):
    bsz, seq, _ = xbc.shape
    ncs = ts // CHUNK
    nseg = seq // ts
    dt3 = dt.reshape(bsz, seq, 128)
    cc3 = cc.reshape(bsz, seq, 128)
    cr4 = cr.reshape(bsz, seq // CHUNK, 128, CHUNK)
    st = pltpu.VMEM((SSD_G, D_STATE, 512), F32)
    y_shape = jax.ShapeDtypeStruct((bsz, seq, SSD_W), BF16)

    def seg_specs(rev):
        ix = (lambda s: nseg - 1 - s) if rev else (lambda s: s)
        return [pl.BlockSpec((None, ts, CONV_DIM), lambda b, s: (b, ix(s), 0)),
                pl.BlockSpec((None, ts, 128), lambda b, s: (b, ix(s), 0)),
                pl.BlockSpec((None, ts, 128), lambda b, s: (b, ix(s), 0)),
                pl.BlockSpec((None, ncs, 128, CHUNK), lambda b, s: (b, ix(s), 0, 0))]

    return pl.pallas_call(
        functools.partial(_ssd_scan_kernel, ncs=ncs),
        out_shape=(y_shape, y_shape),
        grid=(bsz, nseg),
        in_specs=seg_specs(False) + seg_specs(True) + [
            pl.BlockSpec((None, CHUNK, CONV_DIM), lambda b, s: (b, 0, 0)),
            pl.BlockSpec((CHUNK, 128), lambda b, s: (0, 0)),
            pl.BlockSpec((CHUNK, 128), lambda b, s: (0, 0)),
            pl.BlockSpec((1, 128, CHUNK), lambda b, s: (0, 0, 0)),
            pl.BlockSpec((1, SSD_W), lambda b, s: (0, 0))],
        out_specs=(pl.BlockSpec((None, ts, SSD_W), lambda b, s: (b, s, 0)),
                   pl.BlockSpec((None, ts, SSD_W), lambda b, s: (b, nseg - 1 - s, 0))),
        scratch_shapes=[st, st],
        compiler_params=_cp(("parallel", "arbitrary")),
        name="ssd_scan",
    )(xbc, dt3, cc3, cr4, xbc, dt3, cc3, cr4, xbc_m, dt_m, cc_m, cr_m, dsk)


def _out_kernel(o_ref, g_ref, yf_ref, yb_ref, z_ref, mg_ref, x_ref, snw_ref, wa_ref, ws_ref, wo_ref, fnw_ref,
                out_ref):
    g = g_ref[...].astype(F32)
    ag = (o_ref[...].astype(F32) * (g * _sigmoid(g))).astype(BF16)
    ya = jnp.dot(ag, wa_ref[...], preferred_element_type=F32)
    gw = SSD_W // SSD_G
    yn = []
    for grp in range(SSD_G):
        gs = slice(grp * gw, (grp + 1) * gw)
        z = z_ref[:, gs].astype(F32)
        y = yf_ref[:, gs].astype(F32) + yb_ref[:, gs].astype(F32)
        yn.append(_rms(y * (z * _sigmoid(z)), snw_ref[:, gs]).astype(BF16))
    ys = jnp.dot(jnp.concatenate(yn, axis=-1), ws_ref[...], preferred_element_type=F32)
    gate = _sigmoid(mg_ref[...].astype(F32))
    mixed = (gate[:, :D_MODEL] * ya + gate[:, D_MODEL:] * ys).astype(BF16)
    h = x_ref[...] + jnp.dot(mixed, wo_ref[...], preferred_element_type=F32)
    out_ref[...] = _rms(h, fnw_ref[...])


def _outproj(o2, proj, yf2, yb2, x2, snw, wa, ws, wo, fnw, tm):
    rows = x2.shape[0]
    full = lambda shape: pl.BlockSpec(shape, lambda i: (0, 0))
    return pl.pallas_call(
        _out_kernel,
        out_shape=jax.ShapeDtypeStruct((rows, D_MODEL), F32),
        grid=(rows // tm,),
        in_specs=[pl.BlockSpec((tm, ATTN_W), lambda i: (i, 0)),
                  pl.BlockSpec((tm, ATTN_W), lambda i: (i, C_G // ATTN_W)),
                  pl.BlockSpec((tm, SSD_W), lambda i: (i, 0)),
                  pl.BlockSpec((tm, SSD_W), lambda i: (i, 0)),
                  pl.BlockSpec((tm, SSD_W), lambda i: (i, C_Z // SSD_W)),
                  pl.BlockSpec((tm, 2 * D_MODEL), lambda i: (i, C_MG // (2 * D_MODEL))),
                  pl.BlockSpec((tm, D_MODEL), lambda i: (i, 0)),
                  full((1, SSD_W)), full((ATTN_W, D_MODEL)), full((SSD_W, D_MODEL)), full((D_MODEL, D_MODEL)),
                  full((1, D_MODEL))],
        out_specs=pl.BlockSpec((tm, D_MODEL), lambda i: (i, 0)),
        compiler_params=_cp(("parallel",)),
        name="outproj",
    )(o2, proj, yf2, yb2, proj, proj, x2, snw, wa, ws, wo, fnw)


def _rope_tables(pos):
    inv_freq = ROPE_THETA ** (-jnp.arange(0, ROPE, 2, dtype=F32) / ROPE)
    ang = pos.astype(F32)[:, None] * inv_freq[None, :]
    cos, sin = jnp.cos(ang), jnp.sin(ang)
    n = pos.shape[0]
    one = jnp.ones((n, NOPE), F32)
    z16 = jnp.zeros((n, ROPE // 2), F32)
    z32 = jnp.zeros((n, HP - QK_DIM), F32)
    z64 = jnp.zeros((n, NOPE), F32)
    cos_t = jnp.concatenate([one, cos, cos, z32 + 1.0], axis=1)
    sin_p = jnp.concatenate([z64, z16, sin, z32], axis=1)
    sin_m = jnp.concatenate([z64, -sin, z16, z32], axis=1)
    return cos_t, sin_p, sin_m


def _meta_blockdiag(t):
    tv = t[META_PAD:].reshape(N_META, N_HEADS // 2, 2, HP)
    z = jnp.zeros_like(tv[:, :, 0])
    top = jnp.concatenate([tv[:, :, 0], z], axis=-1)
    bot = jnp.concatenate([z, tv[:, :, 1]], axis=-1)
    pad = jnp.zeros((CHUNK - 2 * N_META, N_HEADS // 2, 2 * HP), t.dtype)
    return jnp.swapaxes(jnp.concatenate([top, bot, pad], axis=0), 0, 1)


def _pick(n, prefs):
    for t in prefs:
        if n % t == 0:
            return t
    raise ValueError(f"no tile for {n}")


def kernel(x, meta_tokens, norm_w, w_in, q_norm_w, w_uq, kv_norm_w, w_ukv, w_attn_proj, conv_w, conv_b,
           dt_bias, a_log, d_skip, ssd_norm_w, w_ssd_proj, w_out, final_norm_w):
    bsz, seq, d = x.shape
    assert d == D_MODEL and seq % 128 == 0 and norm_w.shape[0] == 1
    rows = bsz * seq
    zc = lambda n: jnp.zeros((D_MODEL, n), F32)
    w = w_in[0]
    w_p = jnp.concatenate([
        w[:, O_KV:O_KV + KV_RANK], zc(NOPE), w[:, O_KR:O_KR + ROPE], zc(HP - QK_DIM),
        w[:, O_Q:O_Q + Q_RANK], w[:, O_DT:O_DT + 2 * SSD_H], zc(128 - 2 * SSD_H), zc(128),
        w[:, O_G:O_G + ATTN_W], w[:, O_Z:O_Z + SSD_W], w[:, O_XBC:O_XBC + SSD_W],
        w[:, O_MG:O_MG + 2 * D_MODEL], w[:, O_XBC + SSD_W:O_XBC + CONV_DIM]], axis=1).astype(BF16)
    assert w_p.shape[1] == NP
    pad_h = lambda t: jnp.pad(t, ((0, 0), (0, 0), (0, HP - t.shape[-1]))).reshape(t.shape[0], N_HEADS * HP)
    wq = pad_h(w_uq[0].reshape(Q_RANK, N_HEADS, QK_DIM)).astype(BF16)
    assert NOPE + V_DIM == HP
    wkv = w_ukv[0].astype(BF16)
    row = lambda t: t.reshape(1, -1)
    pad128 = lambda t: jnp.pad(t.reshape(1, -1), ((0, 0), (0, 128 - t.size)))

    tm = _pick(rows, (2048, 1024, 512, 256, 128))
    x2 = x.reshape(rows, D_MODEL)
    proj, dt_raw = _inproj(x2, row(norm_w[0]), w_p, tm)
    xm = jnp.concatenate([jnp.zeros((META_PAD, D_MODEL), F32), meta_tokens.astype(F32)], axis=0)
    proj_m, dt_raw_m = _inproj(xm, row(norm_w[0]), w_p, CHUNK)

    tabs = _rope_tables(N_META + jnp.arange(seq))
    tabs_m = _rope_tables(jnp.maximum(jnp.arange(CHUNK) - META_PAD, 0))
    tq2 = _pick(seq, (512, 256, 128))
    q, k, v = _qkv(proj, tabs, row(q_norm_w[0]), row(kv_norm_w[0]), wq, wkv, tq2, seq // tq2)
    _, km, vm = _qkv(proj_m, tabs_m, row(q_norm_w[0]), row(kv_norm_w[0]), wq, wkv, CHUNK, 1)

    r3 = lambda t: t.reshape(bsz, seq, t.shape[-1])
    tq = _pick(seq, (2048, 1024, 512, 256, 128))
    tk = _pick(seq, (1024, 512, 256, 128))
    o = _flash(r3(q), r3(k), r3(v), _meta_blockdiag(km), _meta_blockdiag(vm), tq, tk)

    proj3 = r3(proj)
    cw, cb = conv_w[0], row(conv_b[0])
    xbc = _conv(proj3, proj_m, cw, cb, _pick(seq, (2048, 1024, 512, 256, 128)))
    xbc_m = _conv_meta(proj3, proj_m, cw, cb)
    bias, alog = pad128(dt_bias[0]), pad128(a_log[0])
    dt, cc, cr = _dtprep(dt_raw, bias, alog, _pick(rows // CHUNK, (8, 4, 2, 1)), 0)
    dt_m, cc_m, cr_m = _dtprep(dt_raw_m, bias, alog, 1, META_PAD)
    dsk = row(jnp.repeat(d_skip[0], SSD_P))
    yf, yb = _ssd(xbc, dt, cc, cr, xbc_m, dt_m, cc_m, cr_m, dsk, _pick(seq, (512, 256, 128)))

    out = _outproj(o.reshape(rows, ATTN_W), proj, yf.reshape(rows, SSD_W), yb.reshape(rows, SSD_W), x2,
                   row(ssd_norm_w[0]),
                   w_attn_proj[0].astype(BF16), w_ssd_proj[0].astype(BF16), w_out[0].astype(BF16),
                   row(final_norm_w), _pick(rows, (512, 256, 128)))
    return out.reshape(bsz, seq, D_MODEL)
```

```python
import functools
import math

import jax
import jax.numpy as jnp
from jax import lax
from jax.experimental import pallas as pl
from jax.experimental.pallas import tpu as pltpu

F32 = jnp.float32
BF16 = jnp.bfloat16

D_MODEL = 1024
N_META = 16
N_HEADS = 16
NOPE = 64
ROPE = 32
QK_DIM = NOPE + ROPE
V_DIM = 64
Q_RANK = 384
KV_RANK = 256
ROPE_THETA = 10000.0
ATTN_W = N_HEADS * V_DIM
SSD_W = 2048
SSD_P = 64
SSD_H = SSD_W // SSD_P
SSD_G = 4
SSD_J = SSD_H // SSD_G
D_STATE = 128
CONV_K = 5
GN = SSD_G * D_STATE
CONV_DIM = SSD_W + 2 * GN
CHUNK = 128
EPS = 1e-6
HP = 128
GW = SSD_W // SSD_G
CW = 512
HALO = 16
META_PAD = CHUNK - N_META

C_KV, C_KR, C_Q, C_DT = 0, 256, 384, 768
C_G, C_Z, C_XS, C_MG, C_B, C_C = 1024, 2048, 4096, 6144, 8192, 8704
NP = 9216
O_Q, O_KV, O_KR, O_G, O_Z, O_XBC, O_DT, O_MG = 0, 384, 640, 672, 1696, 3744, 6816, 6880

VMEM_LIMIT = 56 * 1024 * 1024
NEG_INF = float("-inf")
LOG2E = math.log2(math.e)


def _cp(sem):
    return pltpu.CompilerParams(dimension_semantics=sem, vmem_limit_bytes=VMEM_LIMIT)


def _sigmoid(x):
    return 1.0 / (1.0 + jnp.exp(-x))


def _rms(x, w):
    return x * lax.rsqrt(jnp.mean(x * x, axis=-1, keepdims=True) + EPS) * w


def _inproj_kernel(x_ref, nw_ref, w_ref, o_ref, dt_ref, u_ref):
    j = pl.program_id(1)

    @pl.when(j == 0)
    def _():
        u_ref[...] = _rms(x_ref[...], nw_ref[...]).astype(BF16)

    acc = jnp.dot(u_ref[...], w_ref[...], preferred_element_type=F32)
    o_ref[...] = acc.astype(BF16)

    @pl.when(j == 0)
    def _():
        dt_ref[...] = acc[:, C_DT:C_DT + 128]


def _inproj(x2, nw, w, tm, tn=1024):
    rows = x2.shape[0]
    return pl.pallas_call(
        _inproj_kernel,
        out_shape=(jax.ShapeDtypeStruct((rows, NP), BF16),
                   jax.ShapeDtypeStruct((rows, 128), F32)),
        grid=(rows // tm, NP // tn),
        in_specs=[pl.BlockSpec((tm, D_MODEL), lambda i, j: (i, 0)),
                  pl.BlockSpec((1, D_MODEL), lambda i, j: (0, 0)),
                  pl.BlockSpec((D_MODEL, tn), lambda i, j: (0, j))],
        out_specs=(pl.BlockSpec((tm, tn), lambda i, j: (i, j)),
                   pl.BlockSpec((tm, 128), lambda i, j: (i, 0))),
        scratch_shapes=[pltpu.VMEM((tm, D_MODEL), BF16)],
        compiler_params=_cp(("parallel", "arbitrary")),
        name="inproj",
    )(x2, nw, w)


def _qkv_kernel(hd_ref, cos_ref, sp_ref, sm_ref, qnw_ref, kvnw_ref, wq_ref, wkv_ref,
                q_ref, k_ref, v_ref, *, scale):
    cos_t, sin_p, sin_m = cos_ref[...], sp_ref[...], sm_ref[...]

    def rope(t):
        return t * cos_t + pltpu.roll(t, 16, 1) * sin_p + pltpu.roll(t, 112, 1) * sin_m

    kv_lat = hd_ref[:, C_KV:C_KV + KV_RANK].astype(F32)
    kvn = _rms(kv_lat, kvnw_ref[...]).astype(BF16)
    kv = jnp.dot(kvn, wkv_ref[...], preferred_element_type=F32)
    kpe = rope(hd_ref[:, C_KR:C_KR + HP].astype(F32))
    lane = lax.broadcasted_iota(jnp.int32, kpe.shape, 1)
    for h in range(N_HEADS):
        sl = slice(h * HP, (h + 1) * HP)
        k_ref[:, sl] = jnp.where(lane < NOPE, kv[:, sl], kpe).astype(BF16)
        v_ref[:, sl] = jnp.where(lane < NOPE, 1.0, kv[:, sl]).astype(BF16)

    q_lat = hd_ref[:, C_Q:C_Q + Q_RANK].astype(F32)
    qn = _rms(q_lat, qnw_ref[...]).astype(BF16)
    q = jnp.dot(qn, wq_ref[...], preferred_element_type=F32)
    for h in range(N_HEADS):
        sl = slice(h * HP, (h + 1) * HP)
        q_ref[:, sl] = (rope(q[:, sl]) * scale).astype(BF16)


def _qkv(proj, tabs, qnw, kvnw, wq, wkv, tm, tiles_per_seq):
    rows = proj.shape[0]
    w = N_HEADS * HP
    tab_spec = pl.BlockSpec((tm, HP), lambda i: (i % tiles_per_seq, 0))
    out = jax.ShapeDtypeStruct((rows, w), BF16)
    return pl.pallas_call(
        functools.partial(_qkv_kernel, scale=LOG2E * QK_DIM ** -0.5),
        out_shape=(out, out, out),
        grid=(rows // tm,),
        in_specs=[pl.BlockSpec((tm, 1024), lambda i: (i, 0)),
                  tab_spec, tab_spec, tab_spec,
                  pl.BlockSpec((1, Q_RANK), lambda i: (0, 0)),
                  pl.BlockSpec((1, KV_RANK), lambda i: (0, 0)),
                  pl.BlockSpec((Q_RANK, w), lambda i: (0, 0)),
                  pl.BlockSpec((KV_RANK, w), lambda i: (0, 0))],
        out_specs=(pl.BlockSpec((tm, w), lambda i: (i, 0)),) * 3,
        compiler_params=_cp(("parallel",)),
        name="qkv",
    )(proj, *tabs, qnw, kvnw, wq, wkv)


def _flash_kernel(q_ref, k_ref, v_ref, km_ref, vm_ref, o_ref, m_ref, acc_ref, *, tk, nk):
    nt = (((1,), (1,)), ((), ()))
    heads = [slice(hh * HP, (hh + 1) * HP) for hh in range(2)]
    s0 = lax.dot_general(q_ref[...], km_ref[...], nt, preferred_element_type=F32)
    col = lax.broadcasted_iota(jnp.int32, s0.shape, 1)
    s0 = jnp.where(col < 2 * N_META, s0, NEG_INF)
    m0 = jnp.max(jnp.where(col < N_META, s0, NEG_INF), axis=-1, keepdims=True)
    m1 = jnp.max(jnp.where(col >= N_META, s0, NEG_INF), axis=-1, keepdims=True)
    p0 = jnp.exp2(s0 - jnp.where(col < N_META, m0, m1)).astype(BF16)
    acc0 = jnp.dot(p0, vm_ref[...], preferred_element_type=F32)
    for hh, (sl, m) in enumerate(zip(heads, (m0, m1))):
        m_ref[hh] = jnp.broadcast_to(m, s0.shape)
        acc_ref[hh] = acc0[:, sl]

    def body(j, carry):
        off = pl.multiple_of(j * tk, tk)
        ss = [lax.dot_general(q_ref[:, sl], k_ref[pl.ds(off, tk), sl], nt, preferred_element_type=F32)
              for sl in heads]
        for hh, sl in enumerate(heads):
            blocks = [ss[hh][:, c * HP:(c + 1) * HP] for c in range(tk // HP)]
            m = m_ref[hh]
            m_new = jnp.maximum(m, jnp.max(functools.reduce(jnp.maximum, blocks), axis=-1, keepdims=True))
            p = jnp.concatenate([jnp.exp2(blk - m_new) for blk in blocks], axis=-1).astype(BF16)
            pv = jnp.dot(p, v_ref[pl.ds(off, tk), sl], preferred_element_type=F32)
            acc_ref[hh] = jnp.exp2(m - m_new) * acc_ref[hh] + pv
            m_ref[hh] = m_new
        return carry

    lax.fori_loop(0, nk, body, 0, unroll=4)
    a0, a1 = acc_ref[0], acc_ref[1]
    lane = lax.broadcasted_iota(jnp.int32, a0.shape, 1)
    o_ref[...] = jnp.where(lane < V_DIM, pltpu.roll(a0, V_DIM, 1) / a0, a1 / pltpu.roll(a1, V_DIM, 1)).astype(BF16)


def _flash(q, k, v, km, vm, tq, tk):
    bsz, seq, _ = q.shape
    return pl.pallas_call(
        functools.partial(_flash_kernel, tk=tk, nk=seq // tk),
        out_shape=jax.ShapeDtypeStruct((bsz, seq, ATTN_W), BF16),
        grid=(bsz, N_HEADS // 2, seq // tq),
        in_specs=[pl.BlockSpec((None, tq, 2 * HP), lambda b, h, i: (b, i, h)),
                  pl.BlockSpec((None, seq, 2 * HP), lambda b, h, i: (b, 0, h)),
                  pl.BlockSpec((None, seq, 2 * HP), lambda b, h, i: (b, 0, h)),
                  pl.BlockSpec((None, CHUNK, 2 * HP), lambda b, h, i: (h, 0, 0)),
                  pl.BlockSpec((None, CHUNK, 2 * HP), lambda b, h, i: (h, 0, 0))],
        out_specs=pl.BlockSpec((None, tq, 2 * V_DIM), lambda b, h, i: (b, i, h)),
        scratch_shapes=[pltpu.VMEM((2, tq, HP), F32), pltpu.VMEM((2, tq, HP), F32)],
        compiler_params=_cp(("parallel", "parallel", "arbitrary")),
        name="flash",
    )(q, k, v, km, vm)


def _conv_taps(ext_ref, w_ref, b_ref, rows):
    ext = ext_ref[...]
    n = ext.shape[0]
    acc = b_ref[...] + w_ref[CONV_K // 2:CONV_K // 2 + 1, :] * ext[HALO:HALO + rows]
    for t in range(CONV_K):
        if t != CONV_K // 2:
            shifted = pltpu.roll(ext, (CONV_K // 2 - t) % n, 0)
            acc = acc + w_ref[t:t + 1, :] * shifted[HALO:HALO + rows]
    return acc * _sigmoid(acc)


def _conv_kernel(main_ref, prev_ref, next_ref, meta_ref, w_ref, b_ref, o_ref, ext_ref, *, tr, nr):
    r = pl.program_id(2)
    prev = jnp.where(r == 0, meta_ref[CHUNK - HALO:CHUNK, :], prev_ref[...])
    nxt = jnp.where(r == nr - 1, jnp.zeros_like(next_ref[...]), next_ref[...])
    ext_ref[0:HALO, :] = prev.astype(F32)
    ext_ref[HALO:HALO + tr, :] = main_ref[...].astype(F32)
    ext_ref[HALO + tr:2 * HALO + tr, :] = nxt.astype(F32)
    o_ref[...] = _conv_taps(ext_ref, w_ref, b_ref, tr).astype(BF16)


def _conv_col(c):
    n_xs = SSD_W // CW
    return jnp.where(c < n_xs, C_XS // CW + c, C_B // CW + (c - n_xs))


def _conv(proj3, proj_meta, cw, cb, tr):
    bsz, seq, _ = proj3.shape
    nr = seq // tr
    hb = tr // HALO
    return pl.pallas_call(
        functools.partial(_conv_kernel, tr=tr, nr=nr),
        out_shape=jax.ShapeDtypeStruct((bsz, seq, CONV_DIM), BF16),
        grid=(bsz, CONV_DIM // CW, nr),
        in_specs=[pl.BlockSpec((None, tr, CW), lambda b, c, r: (b, r, _conv_col(c))),
                  pl.BlockSpec((None, HALO, CW),
                               lambda b, c, r: (b, jnp.maximum(r * hb - 1, 0), _conv_col(c))),
                  pl.BlockSpec((None, HALO, CW),
                               lambda b, c, r: (b, jnp.minimum((r + 1) * hb, seq // HALO - 1), _conv_col(c))),
                  pl.BlockSpec((CHUNK, CW), lambda b, c, r: (0, _conv_col(c))),
                  pl.BlockSpec((CONV_K, CW), lambda b, c, r: (0, c)),
                  pl.BlockSpec((1, CW), lambda b, c, r: (0, c))],
        out_specs=pl.BlockSpec((None, tr, CW), lambda b, c, r: (b, r, c)),
        scratch_shapes=[pltpu.VMEM((tr + 2 * HALO, CW), F32)],
        compiler_params=_cp(("parallel", "parallel", "arbitrary")),
        name="conv",
    )(proj3, proj3, proj3, proj_meta, cw, cb)


def _conv_meta_kernel(meta_ref, main_ref, w_ref, b_ref, o_ref, ext_ref):
    ext_ref[0:2 * HALO, :] = meta_ref[CHUNK - 2 * HALO:CHUNK, :].astype(F32)
    ext_ref[2 * HALO:3 * HALO, :] = main_ref[...].astype(F32)
    o_ref[0:META_PAD, :] = jnp.zeros((META_PAD, CW), BF16)
    o_ref[META_PAD:CHUNK, :] = _conv_taps(ext_ref, w_ref, b_ref, N_META).astype(BF16)


def _conv_meta(proj3, proj_meta, cw, cb):
    bsz = proj3.shape[0]
    return pl.pallas_call(
        _conv_meta_kernel,
        out_shape=jax.ShapeDtypeStruct((bsz, CHUNK, CONV_DIM), BF16),
        grid=(bsz, CONV_DIM // CW),
        in_specs=[pl.BlockSpec((CHUNK, CW), lambda b, c: (0, _conv_col(c))),
                  pl.BlockSpec((None, HALO, CW), lambda b, c: (b, 0, _conv_col(c))),
                  pl.BlockSpec((CONV_K, CW), lambda b, c: (0, c)),
                  pl.BlockSpec((1, CW), lambda b, c: (0, c))],
        out_specs=pl.BlockSpec((None, CHUNK, CW), lambda b, c: (b, 0, c)),
        scratch_shapes=[pltpu.VMEM((3 * HALO, CW), F32)],
        compiler_params=_cp(("parallel", "parallel")),
        name="conv_meta",
    )(proj_meta, proj3, cw, cb)


def _dt_kernel(raw_ref, bias_ref, alog_ref, dt_ref, cc_ref, cr_ref, *, cpb, valid_from):
    row = lax.broadcasted_iota(jnp.int32, (CHUNK, 128), 0)
    col = lax.broadcasted_iota(jnp.int32, (CHUNK, 128), 1)
    tri = (col <= row).astype(F32)
    triu = (col >= row).astype(F32)
    a = -jnp.exp(alog_ref[...]) * LOG2E
    for c in range(cpb):
        rs = slice(c * CHUNK, (c + 1) * CHUNK)
        x = raw_ref[rs, :] + bias_ref[...]
        dt = jnp.maximum(x, 0.0) + jnp.log(1.0 + jnp.exp(-jnp.abs(x)))
        dt = jnp.where(col < 2 * SSD_H, dt, 0.0)
        if valid_from:
            dt = jnp.where(row >= valid_from, dt, 0.0)
        da = dt * a
        cf = jnp.dot(tri, da, precision=lax.Precision.HIGHEST, preferred_element_type=F32)
        cb = jnp.dot(triu, da, precision=lax.Precision.HIGHEST, preferred_element_type=F32)
        cum = jnp.where(col < SSD_H, cf, cb)
        dt_ref[rs, :] = dt
        cc_ref[rs, :] = cum
        cr_ref[c] = cum.T


def _dtprep(raw, bias, alog, cpb, valid_from):
    rows = raw.shape[0]
    nch = rows // CHUNK
    blk = cpb * CHUNK
    o = jax.ShapeDtypeStruct((rows, 128), F32)
    return pl.pallas_call(
        functools.partial(_dt_kernel, cpb=cpb, valid_from=valid_from),
        out_shape=(o, o, jax.ShapeDtypeStruct((nch, 128, CHUNK), F32)),
        grid=(nch // cpb,),
        in_specs=[pl.BlockSpec((blk, 128), lambda i: (i, 0)),
                  pl.BlockSpec((1, 128), lambda i: (0, 0)),
                  pl.BlockSpec((1, 128), lambda i: (0, 0))],
        out_specs=(pl.BlockSpec((blk, 128), lambda i: (i, 0)),
                   pl.BlockSpec((blk, 128), lambda i: (i, 0)),
                   pl.BlockSpec((cpb, 128, CHUNK), lambda i: (i, 0, 0))),
        compiler_params=_cp(("parallel",)),
        name="dtprep",
    )(raw, bias, alog)


def _ssd_chunk(state, xbc, dtc, cc, cr, g, fwd, want_y):
    row = lax.broadcasted_iota(jnp.int32, (CHUNK, CHUNK), 0)
    col = lax.broadcasted_iota(jnp.int32, (CHUNK, CHUNK), 1)
    keep = (col <= row) if fwd else (col >= row)
    lo = col < SSD_P
    hb = (0 if fwd else SSD_H) + g * SSD_J
    edge = CHUNK - 1 if fwd else 0

    xs = xbc[:, g * GW:(g + 1) * GW].astype(F32)
    bm = xbc[:, SSD_W + g * D_STATE:SSD_W + (g + 1) * D_STATE]
    cm = xbc[:, SSD_W + GN + g * D_STATE:SSD_W + GN + (g + 1) * D_STATE]
    bt = bm.astype(F32).T.astype(BF16)
    if want_y:
        cbm = lax.dot_general(cm, bm, (((1,), (1,)), ((), ())), preferred_element_type=F32)
        yoff = jnp.dot(cm, state.astype(BF16), preferred_element_type=F32)

    def bcl(a, j):
        return jnp.broadcast_to(a[:, hb + j:hb + j + 1], (CHUNK, CHUNK))

    ys, xws, decs = [], [], []
    for kk in range(SSD_J // 2):
        j0, j1 = 2 * kk, 2 * kk + 1
        cc0, cc1 = bcl(cc, j0), bcl(cc, j1)
        cc_pair = jnp.where(lo, cc0, cc1)
        dt_pair = jnp.take_along_axis(dtc, jnp.where(lo, hb + j0, hb + j1), axis=1)
        xdt = xs[:, kk * 128:(kk + 1) * 128] * dt_pair
        tot = cc_pair[edge:edge + 1, :]
        xws.append((xdt * jnp.exp2(tot - cc_pair)).astype(BF16))
        decs.append(jnp.exp2(tot))
        if want_y:
            xdt_b = xdt.astype(BF16)

            def ydiag(ccj, j):
                crj = jnp.broadcast_to(cr[hb + j:hb + j + 1, :], (CHUNK, CHUNK))
                decay = jnp.exp2(jnp.where(keep, ccj - crj, NEG_INF))
                return jnp.dot((cbm * decay).astype(BF16), xdt_b, preferred_element_type=F32)

            yd = jnp.where(lo, ydiag(cc0, j0), ydiag(cc1, j1))
            ys.append(yd + yoff[:, kk * 128:(kk + 1) * 128] * jnp.exp2(cc_pair))
    xw = jnp.concatenate(xws, axis=1)
    dec = jnp.concatenate(decs, axis=1)
    new_state = state * dec + jnp.dot(bt, xw, preferred_element_type=F32)
    return new_state, (jnp.concatenate(ys, axis=1) if want_y else None)


def _ssd_scan_kernel(xf_ref, dtf_ref, ccf_ref, crf_ref, xb_ref, dtb_ref, ccb_ref, crb_ref,
                     xm_ref, dtm_ref, ccm_ref, crm_ref, dsk_ref, yf_ref, yb_ref, stf_ref, stb_ref, *, ncs):
    s = pl.program_id(1)

    @pl.when(s == 0)
    def _():
        stb_ref[...] = jnp.zeros_like(stb_ref)
        for g in range(SSD_G):
            st, _ = _ssd_chunk(jnp.zeros((D_STATE, GW), F32), xm_ref[...], dtm_ref[...], ccm_ref[...],
                               crm_ref[0], g, True, False)
            stf_ref[g] = st

    def body(i, carry):
        cb = ncs - 1 - i
        rf = pl.ds(pl.multiple_of(i * CHUNK, CHUNK), CHUNK)
        rb = pl.ds(pl.multiple_of(cb * CHUNK, CHUNK), CHUNK)
        for g in range(SSD_G):
            gs = slice(g * GW, (g + 1) * GW)
            xf = xf_ref[rf, :]
            st, y = _ssd_chunk(stf_ref[g], xf, dtf_ref[rf, :], ccf_ref[rf, :], crf_ref[i], g, True, True)
            stf_ref[g] = st
            yf_ref[rf, gs] = (y + dsk_ref[:, gs] * xf[:, gs].astype(F32)).astype(BF16)
            st, y = _ssd_chunk(stb_ref[g], xb_ref[rb, :], dtb_ref[rb, :], ccb_ref[rb, :], crb_ref[cb],
                               g, False, True)
            stb_ref[g] = st
            yb_ref[rb, gs] = y.astype(BF16)
        return carry

    lax.fori_loop(0, ncs, body, 0)


def _ssd(xbc, dt, cc, cr, xbc_m, dt_m, cc_m, cr_m, dsk, ts):
    bsz, seq, _ = xbc.shape
    ncs = ts // CHUNK
    nseg = seq // ts
    dt3 = dt.reshape(bsz, seq, 128)
    cc3 = cc.reshape(bsz, seq, 128)
    cr4 = cr.reshape(bsz, seq // CHUNK, 128, CHUNK)
    st = pltpu.VMEM((SSD_G, D_STATE, GW), F32)
    y_shape = jax.ShapeDtypeStruct((bsz, seq, SSD_W), BF16)

    def seg_specs(rev):
        ix = (lambda s: nseg - 1 - s) if rev else (lambda s: s)
        return [pl.BlockSpec((None, ts, CONV_DIM), lambda b, s: (b, ix(s), 0)),
                pl.BlockSpec((None, ts, 128), lambda b, s: (b, ix(s), 0)),
                pl.BlockSpec((None, ts, 128), lambda b, s: (b, ix(s), 0)),
                pl.BlockSpec((None, ncs, 128, CHUNK), lambda b, s: (b, ix(s), 0, 0))]

    return pl.pallas_call(
        functools.partial(_ssd_scan_kernel, ncs=ncs),
        out_shape=(y_shape, y_shape),
        grid=(bsz, nseg),
        in_specs=seg_specs(False) + seg_specs(True) + [
            pl.BlockSpec((None, CHUNK, CONV_DIM), lambda b, s: (b, 0, 0)),
            pl.BlockSpec((CHUNK, 128), lambda b, s: (0, 0)),
            pl.BlockSpec((CHUNK, 128), lambda b, s: (0, 0)),
            pl.BlockSpec((1, 128, CHUNK), lambda b, s: (0, 0, 0)),
            pl.BlockSpec((1, SSD_W), lambda b, s: (0, 0))],
        out_specs=(pl.BlockSpec((None, ts, SSD_W), lambda b, s: (b, s, 0)),
                   pl.BlockSpec((None, ts, SSD_W), lambda b, s: (b, nseg - 1 - s, 0))),
        scratch_shapes=[st, st],
        compiler_params=_cp(("parallel", "arbitrary")),
        name="ssd_scan",
    )(xbc, dt3, cc3, cr4, xbc, dt3, cc3, cr4, xbc_m, dt_m, cc_m, cr_m, dsk)


def _out_kernel(o_ref, g_ref, yf_ref, yb_ref, z_ref, mg_ref, x_ref, snw_ref, wa_ref, ws_ref, wo_ref, fnw_ref,
                out_ref):
    g = g_ref[...].astype(F32)
    ag = (o_ref[...].astype(F32) * (g * _sigmoid(g))).astype(BF16)
    ya = jnp.dot(ag, wa_ref[...], preferred_element_type=F32)
    yn = []
    for grp in range(SSD_G):
        gs = slice(grp * GW, (grp + 1) * GW)
        z = z_ref[:, gs].astype(F32)
        y = yf_ref[:, gs].astype(F32) + yb_ref[:, gs].astype(F32)
        yn.append(_rms(y * (z * _sigmoid(z)), snw_ref[:, gs]).astype(BF16))
    ys = jnp.dot(jnp.concatenate(yn, axis=-1), ws_ref[...], preferred_element_type=F32)
    gate = _sigmoid(mg_ref[...].astype(F32))
    mixed = (gate[:, :D_MODEL] * ya + gate[:, D_MODEL:] * ys).astype(BF16)
    h = x_ref[...] + jnp.dot(mixed, wo_ref[...], preferred_element_type=F32)
    out_ref[...] = _rms(h, fnw_ref[...])


def _outproj(o2, proj, yf2, yb2, x2, snw, wa, ws, wo, fnw, tm):
    rows = x2.shape[0]
    full = lambda shape: pl.BlockSpec(shape, lambda i: (0, 0))
    return pl.pallas_call(
        _out_kernel,
        out_shape=jax.ShapeDtypeStruct((rows, D_MODEL), F32),
        grid=(rows // tm,),
        in_specs=[pl.BlockSpec((tm, ATTN_W), lambda i: (i, 0)),
                  pl.BlockSpec((tm, ATTN_W), lambda i: (i, C_G // ATTN_W)),
                  pl.BlockSpec((tm, SSD_W), lambda i: (i, 0)),
                  pl.BlockSpec((tm, SSD_W), lambda i: (i, 0)),
                  pl.BlockSpec((tm, SSD_W), lambda i: (i, C_Z // SSD_W)),
                  pl.BlockSpec((tm, 2 * D_MODEL), lambda i: (i, C_MG // (2 * D_MODEL))),
                  pl.BlockSpec((tm, D_MODEL), lambda i: (i, 0)),
                  full((1, SSD_W)), full((ATTN_W, D_MODEL)), full((SSD_W, D_MODEL)), full((D_MODEL, D_MODEL)),
                  full((1, D_MODEL))],
        out_specs=pl.BlockSpec((tm, D_MODEL), lambda i: (i, 0)),
        compiler_params=_cp(("parallel",)),
        name="outproj",
    )(o2, proj, yf2, yb2, proj, proj, x2, snw, wa, ws, wo, fnw)


def _rope_tables(pos):
    inv_freq = ROPE_THETA ** (-jnp.arange(0, ROPE, 2, dtype=F32) / ROPE)
    ang = pos.astype(F32)[:, None] * inv_freq[None, :]
    cos, sin = jnp.cos(ang), jnp.sin(ang)
    n = pos.shape[0]
    one = jnp.ones((n, NOPE), F32)
    z16 = jnp.zeros((n, ROPE // 2), F32)
    z32 = jnp.zeros((n, HP - QK_DIM), F32)
    z64 = jnp.zeros((n, NOPE), F32)
    cos_t = jnp.concatenate([one, cos, cos, z32 + 1.0], axis=1)
    sin_p = jnp.concatenate([z64, z16, sin, z32], axis=1)
    sin_m = jnp.concatenate([z64, -sin, z16, z32], axis=1)
    return cos_t, sin_p, sin_m


def _meta_blockdiag(t):
    tv = t[META_PAD:].reshape(N_META, N_HEADS // 2, 2, HP)
    z = jnp.zeros_like(tv[:, :, 0])
    top = jnp.concatenate([tv[:, :, 0], z], axis=-1)
    bot = jnp.concatenate([z, tv[:, :, 1]], axis=-1)
    pad = jnp.zeros((CHUNK - 2 * N_META, N_HEADS // 2, 2 * HP), t.dtype)
    return jnp.swapaxes(jnp.concatenate([top, bot, pad], axis=0), 0, 1)


def _pick(n, prefs):
    for t in prefs:
        if n % t == 0:
            return t
    raise ValueError(f"no tile for {n}")


def kernel(x, meta_tokens, norm_w, w_in, q_norm_w, w_uq, kv_norm_w, w_ukv, w_attn_proj, conv_w, conv_b,
           dt_bias, a_log, d_skip, ssd_norm_w, w_ssd_proj, w_out, final_norm_w):
    bsz, seq, d = x.shape
    assert d == D_MODEL and seq % 128 == 0 and norm_w.shape[0] == 1
    rows = bsz * seq
    zc = lambda n: jnp.zeros((D_MODEL, n), F32)
    w = w_in[0]
    w_p = jnp.concatenate([
        w[:, O_KV:O_KV + KV_RANK], zc(NOPE), w[:, O_KR:O_KR + ROPE], zc(HP - QK_DIM),
        w[:, O_Q:O_Q + Q_RANK], w[:, O_DT:O_DT + 2 * SSD_H], zc(128 - 2 * SSD_H), zc(128),
        w[:, O_G:O_G + ATTN_W], w[:, O_Z:O_Z + SSD_W], w[:, O_XBC:O_XBC + SSD_W],
        w[:, O_MG:O_MG + 2 * D_MODEL], w[:, O_XBC + SSD_W:O_XBC + CONV_DIM]], axis=1).astype(BF16)
    assert w_p.shape[1] == NP
    pad_h = lambda t: jnp.pad(t, ((0, 0), (0, 0), (0, HP - t.shape[-1]))).reshape(t.shape[0], N_HEADS * HP)
    wq = pad_h(w_uq[0].reshape(Q_RANK, N_HEADS, QK_DIM)).astype(BF16)
    assert NOPE + V_DIM == HP
    wkv = w_ukv[0].astype(BF16)
    row = lambda t: t.reshape(1, -1)
    pad128 = lambda t: jnp.pad(t.reshape(1, -1), ((0, 0), (0, 128 - t.size)))

    tm = _pick(rows, (2048, 1024, 512, 256, 128))
    x2 = x.reshape(rows, D_MODEL)
    proj, dt_raw = _inproj(x2, row(norm_w[0]), w_p, tm)
    xm = jnp.concatenate([jnp.zeros((META_PAD, D_MODEL), F32), meta_tokens.astype(F32)], axis=0)
    proj_m, dt_raw_m = _inproj(xm, row(norm_w[0]), w_p, CHUNK)

    tabs = _rope_tables(N_META + jnp.arange(seq))
    tabs_m = _rope_tables(jnp.maximum(jnp.arange(CHUNK) - META_PAD, 0))
    tq2 = _pick(seq, (512, 256, 128))
    q, k, v = _qkv(proj, tabs, row(q_norm_w[0]), row(kv_norm_w[0]), wq, wkv, tq2, seq // tq2)
    _, km, vm = _qkv(proj_m, tabs_m, row(q_norm_w[0]), row(kv_norm_w[0]), wq, wkv, CHUNK, 1)

    r3 = lambda t: t.reshape(bsz, seq, t.shape[-1])
    tq = _pick(seq, (2048, 1024, 512, 256, 128))
    tk = _pick(seq, (1024, 512, 256, 128))
    o = _flash(r3(q), r3(k), r3(v), _meta_blockdiag(km), _meta_blockdiag(vm), tq, tk)

    proj3 = r3(proj)
    cw, cb = conv_w[0], row(conv_b[0])
    xbc = _conv(proj3, proj_m, cw, cb, _pick(seq, (2048, 1024, 512, 256, 128)))
    xbc_m = _conv_meta(proj3, proj_m, cw, cb)
    bias, alog = pad128(dt_bias[0]), pad128(a_log[0])
    dt, cc, cr = _dtprep(dt_raw, bias, alog, _pick(rows // CHUNK, (8, 4, 2, 1)), 0)
    dt_m, cc_m, cr_m = _dtprep(dt_raw_m, bias, alog, 1, META_PAD)
    dsk = row(jnp.repeat(d_skip[0], SSD_P))
    yf, yb = _ssd(xbc, dt, cc, cr, xbc_m, dt_m, cc_m, cr_m, dsk, _pick(seq, (512, 256, 128)))

    out = _outproj(o.reshape(rows, ATTN_W), proj, yf.reshape(rows, SSD_W), yb.reshape(rows, SSD_W), x2,
                   row(ssd_norm_w[0]),
                   w_attn_proj[0].astype(BF16), w_ssd_proj[0].astype(BF16), w_out[0].astype(BF16),
                   row(final_norm_w), _pick(rows, (512, 256, 128)))
    return out.reshape(bsz, seq, D_MODEL)
```
